```python
import math
import jax
import jax.numpy as jnp
from jax import lax
import numpy as np

D_MODEL = 4096
BATCH = 2
SEQ = 8192
DEPTH = 2

GRID_W = 64
CTX_LEN = 256

ATT_HEADS = 8
ATT_DIM = 128
ATT_VDIM = 2 * ATT_DIM
ATT_W = ATT_HEADS * ATT_VDIM
QBLOCK = 128
ROPE_BASE = 10000.0
ROPE_FREQS = ATT_DIM // 4

CONV_W = 1024

SSM_W = 1024
SSM_GROUP = 16
SSM_GROUPS = SSM_W // SSM_GROUP
SSM_STATE = 64
DT_MIN = 1e-3
DT_MAX = 1e-1

N_BRANCH = 3
Q_OFF = 0
K_OFF = Q_OFF + ATT_W
V_OFF = K_OFF + ATT_W
CB_OFF = V_OFF + ATT_W
CC_OFF = CB_OFF + CONV_W
CH_OFF = CC_OFF + CONV_W
SU_OFF = CH_OFF + CONV_W
G_OFF = SU_OFF + SSM_W
IN_W = G_OFF + N_BRANCH * D_MODEL

N_EXPERTS = 64
TOP_K = 8
N_EXPERT_GROUPS = 8
TOPK_GROUPS = 4
EXPERT_FF = 256
SHARED_FF = 256
ROUTED_SCALE = 2.5
MOE_BLOCK = 128

N_MOD = 6
EPS = 1e-6

kernel_name = 'hybrid_diffattn_conv_s5_moe_dit'


def rms_norm(x, g):
    xf = x.astype(jnp.float32)
    y = xf * lax.rsqrt(jnp.mean(xf * xf, axis=-1, keepdims=True) + EPS)
    return (y * g.astype(jnp.float32)).astype(x.dtype)


def adaln(x, g, shift, scale):
    return rms_norm(x, g) * (1.0 + scale) + shift


def modulation(cond, w, b, n):
    m = jax.nn.silu(cond) @ w[:, :n * D_MODEL] + b[:n * D_MODEL]
    return m.reshape(m.shape[:-1] + (n, D_MODEL))


def axial_rope(n_lat):
    rows = n_lat // GRID_W
    row = jnp.repeat(jnp.arange(rows), GRID_W).astype(jnp.float32)
    col = jnp.tile(jnp.arange(GRID_W), rows).astype(jnp.float32)
    inv = ROPE_BASE ** (-jnp.arange(ROPE_FREQS, dtype=jnp.float32) / ROPE_FREQS)
    ang = jnp.concatenate([row[:, None] * inv, col[:, None] * inv], axis=-1)
    ang = ang.reshape(n_lat, 2, ROPE_FREQS)
    return jnp.cos(ang), jnp.sin(ang)


def apply_rope(x, cos, sin):
    xr = x.astype(jnp.float32).reshape(x.shape[:-1] + (2, 2, ROPE_FREQS))
    x1 = xr[..., 0, :]
    x2 = xr[..., 1, :]
    out = jnp.stack([x1 * cos - x2 * sin, x1 * sin + x2 * cos], axis=-2)
    return out.reshape(x.shape).astype(x.dtype)


def split_qk(t):
    b, n, _ = t.shape
    return t.reshape(b, n, ATT_HEADS, 2, ATT_DIM).transpose(0, 2, 3, 1, 4)


def split_v(t):
    b, n, _ = t.shape
    return t.reshape(b, n, ATT_HEADS, ATT_VDIM).transpose(0, 2, 1, 3)


def diff_softmax(q, k, v, lam):
    s = jnp.einsum('bhmqd,bhmkd->bhmqk', q, k).astype(jnp.float32) * (ATT_DIM ** -0.5)
    p = jax.nn.softmax(s, axis=-1)
    w = p[:, :, 0] - lam * p[:, :, 1]
    return jnp.einsum('bhqk,bhkv->bhqv', w.astype(v.dtype), v)


def merge_heads(o, att_g, lam_init):
    b, h, n, _ = o.shape
    o = rms_norm(o, att_g) * (1.0 - lam_init)
    return o.transpose(0, 2, 1, 3).reshape(b, n, ATT_W)


def short_conv(u, w):
    up = jnp.pad(u, ((0, 0), (1, 1), (0, 0)))
    return w[0] * up[:, :-2] + w[1] * up[:, 1:-1] + w[2] * up[:, 2:]


def ssm_discretize(a_re, a_im, log_dt, b_re, b_im):
    dt = jnp.exp(log_dt.astype(jnp.float32))[:, None]
    lr = jnp.minimum(a_re.astype(jnp.float32), -1e-4)
    li = a_im.astype(jnp.float32)
    mag = jnp.exp(lr * dt)
    ar = mag * jnp.cos(li * dt)
    ai = mag * jnp.sin(li * dt)
    den = lr * lr + li * li
    cr = ((ar - 1.0) * lr + ai * li) / den
    ci = (ai * lr - (ar - 1.0) * li) / den
    bre = b_re.astype(jnp.float32)
    bim = b_im.astype(jnp.float32)
    br = cr[..., None] * bre - ci[..., None] * bim
    bi = cr[..., None] * bim + ci[..., None] * bre
    return ar, ai, br, bi


def cplx_combine(e1, e2):
    a1r, a1i, b1r, b1i = e1
    a2r, a2i, b2r, b2i = e2
    return (a1r * a2r - a1i * a2i,
            a1r * a2i + a1i * a2r,
            a2r * b1r - a2i * b1i + b2r,
            a2r * b1i + a2i * b1r + b2i)


def ssm_scan(u, ar, ai, br, bi, h0, reverse):
    xr = jnp.einsum('blgi,gpi->blgp', u, br)
    xi = jnp.einsum('blgi,gpi->blgp', u, bi)
    if h0 is not None:
        t0 = -1 if reverse else 0
        h0r, h0i = h0
        xr = xr.at[:, t0].add(ar * h0r - ai * h0i)
        xi = xi.at[:, t0].add(ar * h0i + ai * h0r)
    n = u.shape[1]
    a_r = jnp.broadcast_to(ar, (1, n) + ar.shape)
    a_i = jnp.broadcast_to(ai, (1, n) + ai.shape)
    _, _, hr, hi = lax.associative_scan(cplx_combine, (a_r, a_i, xr, xi), reverse=reverse, axis=1)
    return hr, hi


def ssm_readout(hr, hi, c_re, c_im):
    return (jnp.einsum('blgp,gip->blgi', hr, c_re.astype(jnp.float32))
            - jnp.einsum('blgp,gip->blgi', hi, c_im.astype(jnp.float32)))


def s5_glu(y, w, b):
    g = jax.nn.gelu(y)
    return g * jax.nn.sigmoid(g @ w + b)


def ssm_branch(u_lat, u_ctx, a_re, a_im, log_dt, b_re, b_im, c_re, c_im, ssm_d, glu_w, glu_b, need_ctx):
    bsz, n_lat, _ = u_lat.shape
    n_ctx = u_ctx.shape[1]
    ul = u_lat.astype(jnp.float32).reshape(bsz, n_lat, SSM_GROUPS, SSM_GROUP)
    uc = u_ctx.astype(jnp.float32).reshape(bsz, n_ctx, SSM_GROUPS, SSM_GROUP)
    dd = ssm_d.astype(jnp.float32).reshape(SSM_GROUPS, SSM_GROUP)
    y_l = dd * ul
    y_c = dd * uc if need_ctx else None
    for d, rev in ((0, False), (1, True)):
        ar, ai, br, bi = ssm_discretize(a_re[d], a_im[d], log_dt[d], b_re[d], b_im[d])
        hcr, hci = ssm_scan(uc, ar, ai, br, bi, None, rev)
        last = 0 if rev else -1
        hlr, hli = ssm_scan(ul, ar, ai, br, bi, (hcr[:, last], hci[:, last]), rev)
        y_l = y_l + ssm_readout(hlr, hli, c_re[d], c_im[d])
        if need_ctx:
            y_c = y_c + ssm_readout(hcr, hci, c_re[d], c_im[d])
    out_l = s5_glu(y_l.reshape(bsz, n_lat, SSM_W).astype(u_lat.dtype), glu_w, glu_b)
    out_c = s5_glu(y_c.reshape(bsz, n_ctx, SSM_W).astype(u_ctx.dtype), glu_w, glu_b) if need_ctx else None
    return out_l, out_c


def branch_merge(gates, att, conv, ssm, p_att, p_conv, p_ssm, w_o):
    g = jax.nn.sigmoid(gates.reshape(gates.shape[:-1] + (N_BRANCH, D_MODEL)))
    m = g[..., 0, :] * (att @ p_att) + g[..., 1, :] * (conv @ p_conv) + g[..., 2, :] * (ssm @ p_ssm)
    return m @ w_o


def token_mixer(h_lat, h_ctx, w_in, lam_vec, att_g, conv_w, a_re, a_im, log_dt, b_re, b_im,
                c_re, c_im, ssm_d, glu_w, glu_b, p_att, p_conv, p_ssm, w_o, lam_init, need_ctx):
    bsz, n_lat, _ = h_lat.shape
    z = h_lat @ w_in

    def lat_cols(off, width):
        return z[..., off:off + width]

    def ctx_cols(off, width):
        return h_ctx @ w_in[:, off:off + width]

    lv = lam_vec.astype(jnp.float32)
    lam = jnp.exp(jnp.sum(lv[0] * lv[1])) - jnp.exp(jnp.sum(lv[2] * lv[3])) + lam_init
    cos, sin = axial_rope(n_lat)
    q_l = apply_rope(split_qk(lat_cols(Q_OFF, ATT_W)), cos, sin)
    k_l = apply_rope(split_qk(lat_cols(K_OFF, ATT_W)), cos, sin)
    v_l = split_v(lat_cols(V_OFF, ATT_W))
    k_c = split_qk(ctx_cols(K_OFF, ATT_W))
    v_c = split_v(ctx_cols(V_OFF, ATT_W))
    k_all = jnp.concatenate([k_c, k_l], axis=3)
    v_all = jnp.concatenate([v_c, v_l], axis=2)
    n_blk = n_lat // QBLOCK
    q_blocks = q_l.reshape(bsz, ATT_HEADS, 2, n_blk, QBLOCK, ATT_DIM).transpose(3, 0, 1, 2, 4, 5)
    o_blocks = lax.map(lambda qb: diff_softmax(qb, k_all, v_all, lam), q_blocks)
    o_l = o_blocks.transpose(1, 2, 0, 3, 4).reshape(bsz, ATT_HEADS, n_lat, ATT_VDIM)
    att_l = merge_heads(o_l, att_g, lam_init)

    conv_l = lat_cols(CB_OFF, CONV_W) * short_conv(lat_cols(CC_OFF, CONV_W) * lat_cols(CH_OFF, CONV_W), conv_w)

    ssm_l, ssm_c = ssm_branch(lat_cols(SU_OFF, SSM_W), ctx_cols(SU_OFF, SSM_W), a_re, a_im, log_dt,
                              b_re, b_im, c_re, c_im, ssm_d, glu_w, glu_b, need_ctx)

    out_l = branch_merge(lat_cols(G_OFF, N_BRANCH * D_MODEL), att_l, conv_l, ssm_l, p_att, p_conv, p_ssm, w_o)
    if not need_ctx:
        return out_l, None
    q_c = split_qk(ctx_cols(Q_OFF, ATT_W))
    att_c = merge_heads(diff_softmax(q_c, k_c, v_c, lam), att_g, lam_init)
    conv_c = ctx_cols(CB_OFF, CONV_W) * short_conv(ctx_cols(CC_OFF, CONV_W) * ctx_cols(CH_OFF, CONV_W), conv_w)
    out_c = branch_merge(ctx_cols(G_OFF, N_BRANCH * D_MODEL), att_c, conv_c, ssm_c, p_att, p_conv, p_ssm, w_o)
    return out_l, out_c


def moe_block(h, router_w, router_b, w_gate, w_up, w_down, ws_gate, ws_up, ws_down):
    n = h.shape[0]
    scores = jax.nn.sigmoid((h @ router_w).astype(jnp.float32))
    sel = scores + router_b.astype(jnp.float32)
    grp = sel.reshape(n, N_EXPERT_GROUPS, N_EXPERTS // N_EXPERT_GROUPS)
    gscore = jnp.sum(lax.top_k(grp, 2)[0], axis=-1)
    _, gidx = lax.top_k(gscore, TOPK_GROUPS)
    gmask = jnp.sum(jax.nn.one_hot(gidx, N_EXPERT_GROUPS, dtype=jnp.float32), axis=-2)
    emask = jnp.repeat(gmask, N_EXPERTS // N_EXPERT_GROUPS, axis=-1)
    _, eidx = lax.top_k(jnp.where(emask > 0, sel, -jnp.inf), TOP_K)
    w = jnp.take_along_axis(scores, eidx, axis=-1)
    w = w / jnp.sum(w, axis=-1, keepdims=True) * ROUTED_SCALE
    gate = jnp.sum(jax.nn.one_hot(eidx, N_EXPERTS, dtype=jnp.float32) * w[..., None], axis=-2)
    hid = jax.nn.silu(jnp.einsum('nd,edf->nef', h, w_gate)) * jnp.einsum('nd,edf->nef', h, w_up)
    routed = jnp.einsum('nef,efd->nd', hid * gate[..., None].astype(h.dtype), w_down)
    shared = (jax.nn.silu(h @ ws_gate) * (h @ ws_up)) @ ws_down
    return routed + shared


def moe(h, router_w, router_b, w_gate, w_up, w_down, ws_gate, ws_up, ws_down):
    b, n, d = h.shape
    hb = h.reshape(b * n // MOE_BLOCK, MOE_BLOCK, d)
    out = lax.map(lambda t: moe_block(t, router_w, router_b, w_gate, w_up, w_down, ws_gate, ws_up, ws_down), hb)
    return out.reshape(b, n, d)


def setup_inputs(seed: int = 0) -> dict:
    key = jax.random.key(seed)
    k = jax.random.split(key, 34)
    f32 = jnp.float32
    L, D, G, P, I = DEPTH, D_MODEL, SSM_GROUPS, SSM_STATE, SSM_GROUP

    def nrm(kk, shape, scale):
        return jax.random.normal(kk, shape, f32) * scale

    a_im_base = math.pi * jnp.arange(P, dtype=f32)
    return {
        'x': nrm(k[0], (BATCH, SEQ, D), 1.0),
        'c': nrm(k[1], (BATCH, D), 1.0),
        'ctx': nrm(k[2], (BATCH, CTX_LEN, D), 1.0),
        'c_ctx': nrm(k[3], (D,), 1.0),
        'ada_w': nrm(k[4], (L, D, N_MOD * D), 0.5 * D ** -0.5),
        'ada_b': nrm(k[5], (L, N_MOD * D), 0.01),
        'norm_g': 1.0 + nrm(k[6], (L, 4, D), 0.1),
        'w_in': nrm(k[7], (L, D, IN_W), D ** -0.5),
        'lam_vec': nrm(k[8], (L, 4, ATT_DIM), 0.1),
        'att_g': 1.0 + nrm(k[9], (L, ATT_VDIM), 0.1),
        'conv_w': nrm(k[10], (L, 3, CONV_W), 3 ** -0.5),
        'ssm_a_re': -0.5 + nrm(k[11], (L, 2, G, P), 0.01),
        'ssm_a_im': a_im_base + nrm(k[12], (L, 2, G, P), 0.01),
        'ssm_log_dt': jax.random.uniform(k[13], (L, 2, G), f32, math.log(DT_MIN), math.log(DT_MAX)),
        'ssm_b_re': nrm(k[14], (L, 2, G, P, I), (2 * I) ** -0.5),
        'ssm_b_im': nrm(k[15], (L, 2, G, P, I), (2 * I) ** -0.5),
        'ssm_c_re': nrm(k[16], (L, 2, G, I, P), P ** -0.5),
        'ssm_c_im': nrm(k[17], (L, 2, G, I, P), P ** -0.5),
        'ssm_d': nrm(k[18], (L, SSM_W), 0.5),
        'glu_w': nrm(k[19], (L, SSM_W, SSM_W), SSM_W ** -0.5),
        'glu_b': nrm(k[20], (L, SSM_W), 0.01),
        'p_att': nrm(k[21], (L, ATT_W, D), ATT_W ** -0.5),
        'p_conv': nrm(k[22], (L, CONV_W, D), CONV_W ** -0.5),
        'p_ssm': nrm(k[23], (L, SSM_W, D), SSM_W ** -0.5),
        'w_o': nrm(k[24], (L, D, D), D ** -0.5),
        'router_w': nrm(k[25], (L, D, N_EXPERTS), D ** -0.5),
        'router_b': nrm(k[26], (L, N_EXPERTS), 0.01),
        'exp_w_gate': nrm(k[27], (L, N_EXPERTS, D, EXPERT_FF), D ** -0.5),
        'exp_w_up': nrm(k[28], (L, N_EXPERTS, D, EXPERT_FF), D ** -0.5),
        'exp_w_down': nrm(k[29], (L, N_EXPERTS, EXPERT_FF, D), EXPERT_FF ** -0.5),
        'sh_w_gate': nrm(k[30], (L, D, SHARED_FF), D ** -0.5),
        'sh_w_up': nrm(k[31], (L, D, SHARED_FF), D ** -0.5),
        'sh_w_down': nrm(k[32], (L, SHARED_FF, D), SHARED_FF ** -0.5),
    }


def reference(x, c, ctx, c_ctx, ada_w, ada_b, norm_g, w_in, lam_vec, att_g, conv_w,
              ssm_a_re, ssm_a_im, ssm_log_dt, ssm_b_re, ssm_b_im, ssm_c_re, ssm_c_im, ssm_d,
              glu_w, glu_b, p_att, p_conv, p_ssm, w_o, router_w, router_b,
              exp_w_gate, exp_w_up, exp_w_down, sh_w_gate, sh_w_up, sh_w_down):
    n_ctx = ctx.shape[1]
    lat, cx = x, ctx
    for i in range(DEPTH):
        need_ctx = i < DEPTH - 1
        lam_init = 0.8 - 0.6 * math.exp(-0.3 * i)
        m_lat = modulation(c, ada_w[i], ada_b[i], N_MOD)[:, :, None, :]
        m_ctx = modulation(c_ctx, ada_w[i], ada_b[i], N_MOD if need_ctx else 2)
        g = norm_g[i]
        h_l = adaln(lat, g[0], m_lat[:, 0], m_lat[:, 1])
        h_c = adaln(cx, g[0], m_ctx[0], m_ctx[1])
        o_l, o_c = token_mixer(h_l, h_c, w_in[i], lam_vec[i], att_g[i], conv_w[i],
                               ssm_a_re[i], ssm_a_im[i], ssm_log_dt[i], ssm_b_re[i], ssm_b_im[i],
                               ssm_c_re[i], ssm_c_im[i], ssm_d[i], glu_w[i], glu_b[i],
                               p_att[i], p_conv[i], p_ssm[i], w_o[i], lam_init, need_ctx)
        lat = lat + m_lat[:, 2] * rms_norm(o_l, g[1])
        h2_l = adaln(lat, g[2], m_lat[:, 3], m_lat[:, 4])
        if need_ctx:
            cx = cx + m_ctx[2] * rms_norm(o_c, g[1])
            h2_c = adaln(cx, g[2], m_ctx[3], m_ctx[4])
            f = moe(jnp.concatenate([h2_c, h2_l], axis=1), router_w[i], router_b[i], exp_w_gate[i],
                    exp_w_up[i], exp_w_down[i], sh_w_gate[i], sh_w_up[i], sh_w_down[i])
            f_c = f[:, :n_ctx]
            f_l = f[:, n_ctx:]
            cx = cx + m_ctx[5] * rms_norm(f_c, g[3])
        else:
            f_l = moe(h2_l, router_w[i], router_b[i], exp_w_gate[i], exp_w_up[i], exp_w_down[i],
                      sh_w_gate[i], sh_w_up[i], sh_w_down[i])
        lat = lat + m_lat[:, 5] * rms_norm(f_l, g[3])
    return lat
```

```python
import functools
import math

import jax
import jax.numpy as jnp
from jax import lax
from jax.experimental import pallas as pl
from jax.experimental.pallas import tpu as pltpu

F32 = jnp.float32
BF16 = jnp.bfloat16

GRID_W = 64
ROPE_BASE = 10000.0
ATT_DIM = 128
ATT_VDIM = 2 * ATT_DIM
N_BRANCH = 3
TOP_K = 8
N_EXPERT_GROUPS = 8
TOPK_GROUPS = 4
ROUTED_SCALE = 2.5
N_MOD = 6
EPS = 1e-6
SSM_CHUNK = 16

V7X_LANES = 128
V7X_VMEM_LIMIT_BYTES = 56 * 1024 * 1024
ROW_BLOCK = 256


def _divisor(n, target, mult):
    best = None
    for d in range(mult, min(n, target) + 1, mult):
        if n % d == 0:
            best = d
    return best if best is not None else n


def _params(sem):
    return pltpu.CompilerParams(dimension_semantics=sem, vmem_limit_bytes=V7X_VMEM_LIMIT_BYTES)


def _mod_kernel(s_ref, w_ref, b_ref, o_ref, *, rows, kc):
    s = s_ref[...]
    s = s * jax.nn.sigmoid(s)
    o_ref[...] = jnp.zeros(o_ref.shape, F32)
    d = w_ref.shape[0]
    for r in range(rows):
        acc = b_ref[...]
        for k0 in range(0, d, kc):
            acc = acc + jnp.sum(w_ref[k0:k0 + kc, :] * s[k0:k0 + kc, r:r + 1], axis=0, keepdims=True)
        o_ref[r:r + 1, :] = acc


def modulation(cond_cols, w, b, rows):
    d, n = w.shape
    tn = _divisor(n, 512, V7X_LANES)
    kc = _divisor(d, 512, 8)
    return pl.pallas_call(
        functools.partial(_mod_kernel, rows=rows, kc=kc),
        grid=(n // tn,),
        in_specs=[pl.BlockSpec((d, 8), lambda j: (0, 0)),
                  pl.BlockSpec((d, tn), lambda j: (0, j)),
                  pl.BlockSpec((1, tn), lambda j: (0, j))],
        out_specs=pl.BlockSpec((8, tn), lambda j: (0, j)),
        out_shape=jax.ShapeDtypeStruct((8, n), F32),
        compiler_params=_params(("arbitrary",)),
        name="modulation",
    )(cond_cols, w, b.reshape(1, n))


def _rms(x, g):
    return x * lax.rsqrt(jnp.mean(x * x, axis=-1, keepdims=True) + EPS) * g


def _resid_adaln_kernel(*refs, has_branch, has_h, has_router, post_idx, gate_idx, pre_idx, shift_idx, scale_idx):
    it = iter(refs)
    x_ref = next(it)
    o_ref = next(it) if has_branch else None
    mod_ref = next(it)
    g_ref = next(it)
    rwh_ref = next(it) if has_router else None
    rwl_ref = next(it) if has_router else None
    xout_ref = next(it) if has_branch else None
    h_ref = next(it) if has_h else None
    lg_ref = next(it) if has_router else None

    x = x_ref[...]
    if has_branch:
        o = o_ref[...].astype(F32)
        x = x + mod_ref[gate_idx:gate_idx + 1, :] * _rms(o, g_ref[post_idx:post_idx + 1, :])
        xout_ref[...] = x
    if has_h:
        h = _rms(x, g_ref[pre_idx:pre_idx + 1, :])
        h = h * (1.0 + mod_ref[scale_idx:scale_idx + 1, :]) + mod_ref[shift_idx:shift_idx + 1, :]
        h_ref[...] = h.astype(BF16)
        if has_router:
            h_hi = h.astype(BF16)
            h_lo = (h - h_hi.astype(F32)).astype(BF16)
            nt = (((1,), (1,)), ((), ()))
            lg = lax.dot_general(rwh_ref[...], h_hi, nt, preferred_element_type=F32)
            lg = lg + lax.dot_general(rwh_ref[...], h_lo, nt, preferred_element_type=F32)
            lg = lg + lax.dot_general(rwl_ref[...], h_hi, nt, preferred_element_type=F32)
            lg_ref[...] = lg


def resid_adaln(x, branch, mod, gains, *, ctx_len, post_idx=0, gate_idx=0, pre_idx=None,
                shift_idx=0, scale_idx=0, router=None, latent_only_out=False):
    bsz, t, d = x.shape
    rb = _divisor(math.gcd(ctx_len, t - ctx_len), ROW_BLOCK, 8)
    nctx = ctx_len // rb
    has_branch = branch is not None
    has_h = pre_idx is not None
    has_router = router is not None
    off = nctx if latent_only_out else 0
    nblk = t // rb - off

    def row_map(b, i):
        return (b, i + off, 0)

    in_specs = [pl.BlockSpec((None, rb, d), row_map)]
    args = [x]
    if has_branch:
        in_specs.append(pl.BlockSpec((None, rb, d), row_map))
        args.append(branch)
    in_specs.append(pl.BlockSpec((None, None, N_MOD, d), lambda b, i: (b, jnp.where(i + off >= nctx, 1, 0), 0, 0)))
    args.append(mod)
    in_specs.append(pl.BlockSpec(gains.shape, lambda b, i: (0, 0)))
    args.append(gains)
    if has_router:
        rw_hi, rw_lo = router
        in_specs += [pl.BlockSpec(rw_hi.shape, lambda b, i: (0, 0)), pl.BlockSpec(rw_lo.shape, lambda b, i: (0, 0))]
        args += [rw_hi, rw_lo]
    out_specs, out_shape = [], []
    if has_branch:
        out_specs.append(pl.BlockSpec((None, rb, d), lambda b, i: (b, i, 0)))
        out_shape.append(jax.ShapeDtypeStruct((bsz, nblk * rb, d), F32))
    if has_h:
        out_specs.append(pl.BlockSpec((None, rb, d), lambda b, i: (b, i, 0)))
        out_shape.append(jax.ShapeDtypeStruct((bsz, t, d), BF16))
    if has_router:
        n_e = router[0].shape[0]
        out_specs.append(pl.BlockSpec((n_e, rb), lambda b, i: (0, b * nblk + i)))
        out_shape.append(jax.ShapeDtypeStruct((n_e, bsz * t), F32))
    return pl.pallas_call(
        functools.partial(_resid_adaln_kernel, has_branch=has_branch, has_h=has_h, has_router=has_router,
                          post_idx=post_idx, gate_idx=gate_idx, pre_idx=pre_idx, shift_idx=shift_idx,
                          scale_idx=scale_idx),
        grid=(bsz, nblk),
        in_specs=in_specs,
        out_specs=out_specs,
        out_shape=out_shape,
        compiler_params=_params(("parallel", "parallel")),
        name="resid_adaln",
    )(*args)


def _mm_kernel(a_ref, b_ref, o_ref):
    o_ref[...] = jnp.dot(a_ref[...], b_ref[...], preferred_element_type=F32).astype(o_ref.dtype)


def _mm_acc_kernel(a_ref, b_ref, o_ref, acc_ref):
    @pl.when(pl.program_id(2) == 0)
    def _():
        acc_ref[...] = jnp.zeros(acc_ref.shape, F32)

    acc_ref[...] += jnp.dot(a_ref[...], b_ref[...], preferred_element_type=F32)

    @pl.when(pl.program_id(2) == pl.num_programs(2) - 1)
    def _():
        o_ref[...] = acc_ref[...].astype(o_ref.dtype)


def matmul(a, b, out_dtype=BF16, tm_target=768, tn_target=1024, tk_target=None):
    m, k = a.shape
    _, n = b.shape
    tm = _divisor(m, tm_target, 16)
    tn = _divisor(n, tn_target, V7X_LANES)
    if tk_target is None or tk_target >= k:
        return pl.pallas_call(
            _mm_kernel,
            grid=(m // tm, n // tn),
            in_specs=[pl.BlockSpec((tm, k), lambda i, j: (i, 0)), pl.BlockSpec((k, tn), lambda i, j: (0, j))],
            out_specs=pl.BlockSpec((tm, tn), lambda i, j: (i, j)),
            out_shape=jax.ShapeDtypeStruct((m, n), out_dtype),
            compiler_params=_params(("parallel", "parallel")),
            name="matmul",
        )(a, b)
    tk = _divisor(k, tk_target, V7X_LANES)
    return pl.pallas_call(
        _mm_acc_kernel,
        grid=(m // tm, n // tn, k // tk),
        in_specs=[pl.BlockSpec((tm, tk), lambda i, j, l: (i, l)), pl.BlockSpec((tk, tn), lambda i, j, l: (l, j))],
        out_specs=pl.BlockSpec((tm, tn), lambda i, j, l: (i, j)),
        out_shape=jax.ShapeDtypeStruct((m, n), out_dtype),
        scratch_shapes=[pltpu.VMEM((tm, tn), F32)],
        compiler_params=_params(("parallel", "parallel", "arbitrary")),
        name="matmul_ksplit",
    )(a, b)


def rope_tables(ctx_len, n_lat):
    nf = ATT_DIM // 4
    rows = n_lat // GRID_W
    row = jnp.repeat(jnp.arange(rows), GRID_W).astype(F32)
    col = jnp.tile(jnp.arange(GRID_W), rows).astype(F32)
    inv = ROPE_BASE ** (-jnp.arange(nf, dtype=F32) / nf)
    ang_r = row[:, None] * inv
    ang_c = col[:, None] * inv
    zero = jnp.zeros_like(ang_r)
    cos = jnp.concatenate([jnp.cos(ang_r), jnp.cos(ang_r), jnp.cos(ang_c), jnp.cos(ang_c)], axis=-1)
    sa = jnp.concatenate([-jnp.sin(ang_r), zero, -jnp.sin(ang_c), zero], axis=-1)
    sb = jnp.concatenate([zero, jnp.sin(ang_r), zero, jnp.sin(ang_c)], axis=-1)
    pad = lambda tbl, v: jnp.concatenate([jnp.full((ctx_len, ATT_DIM), v, F32), tbl], axis=0)
    return pad(cos, 1.0), pad(sa, 0.0), pad(sb, 0.0)


def _rope_kernel(q_ref, k_ref, cos_ref, sa_ref, sb_ref, qo_ref, ko_ref, *, q_scale):
    cos, sa, sb = cos_ref[...], sa_ref[...], sb_ref[...]
    quarter = ATT_DIM // 4
    for src, dst, scale in ((q_ref, qo_ref, q_scale), (k_ref, ko_ref, 1.0)):
        for g0 in range(0, src.shape[-1], ATT_DIM):
            x = src[:, g0:g0 + ATT_DIM].astype(F32)
            y = x * cos + pltpu.roll(x, ATT_DIM - quarter, 1) * sa + pltpu.roll(x, quarter, 1) * sb
            dst[:, g0:g0 + ATT_DIM] = (y * scale).astype(dst.dtype)


def rope_qk(z3, tables, att_w, ctx_len):
    bsz, t, _ = z3.shape
    rb = _divisor(math.gcd(ctx_len, t - ctx_len), ROW_BLOCK, 8)
    tbl_spec = pl.BlockSpec((rb, ATT_DIM), lambda b, i: (i, 0))
    out_spec = pl.BlockSpec((None, rb, att_w), lambda b, i: (b, i, 0))
    return pl.pallas_call(
        functools.partial(_rope_kernel, q_scale=ATT_DIM ** -0.5),
        grid=(bsz, t // rb),
        in_specs=[pl.BlockSpec((None, rb, att_w), lambda b, i: (b, i, 0)),
                  pl.BlockSpec((None, rb, att_w), lambda b, i: (b, i, 1)),
                  tbl_spec, tbl_spec, tbl_spec],
        out_specs=[out_spec, out_spec],
        out_shape=[jax.ShapeDtypeStruct((bsz, t, att_w), BF16)] * 2,
        compiler_params=_params(("parallel", "parallel")),
        name="rope_qk",
    )(z3, z3, *tables)


def _attn_kernel(lam_ref, g_ref, q_ref, k_ref, v_ref, prev_ref, o_ref, acc1_ref, acc2_ref, *, tk, lam_init):
    del prev_ref
    q = q_ref[...]
    q1, q2 = q[:, :ATT_DIM], q[:, ATT_DIM:]
    lv = lam_ref[...]
    lam = (jnp.exp(jnp.sum(lv[0:1] * lv[1:2], axis=-1, keepdims=True))
           - jnp.exp(jnp.sum(lv[2:3] * lv[3:4], axis=-1, keepdims=True)) + lam_init)
    tq = q.shape[0]
    nt = (((1,), (1,)), ((), ()))
    acc1_ref[...] = jnp.zeros(acc1_ref.shape, F32)
    acc2_ref[...] = jnp.zeros(acc2_ref.shape, F32)

    def one_map(qm, km, vc, acc_ref, m, l):
        s = lax.dot_general(qm, km, nt, preferred_element_type=F32)
        m_new = jnp.maximum(m, jnp.max(s, axis=-1, keepdims=True))
        alpha = jnp.exp(m - m_new)
        p = jnp.exp(s - m_new)
        l_new = alpha * l + jnp.sum(p, axis=-1, keepdims=True)
        acc_ref[...] = alpha * acc_ref[...] + jnp.dot(p.astype(BF16), vc, preferred_element_type=F32)
        return m_new, l_new

    def body(c, carry):
        m1, l1, m2, l2 = carry
        r0 = pl.multiple_of(c * tk, tk)
        kc = k_ref[pl.ds(r0, tk), :]
        vc = v_ref[pl.ds(r0, tk), :]
        m1, l1 = one_map(q1, kc[:, :ATT_DIM], vc, acc1_ref, m1, l1)
        m2, l2 = one_map(q2, kc[:, ATT_DIM:], vc, acc2_ref, m2, l2)
        return m1, l1, m2, l2

    neg = jnp.full((tq, 1), -jnp.inf, F32)
    zero = jnp.zeros((tq, 1), F32)
    _, l1, _, l2 = lax.fori_loop(0, k_ref.shape[0] // tk, body, (neg, zero, neg, zero))
    o = acc1_ref[...] / l1 - lam * (acc2_ref[...] / l2)
    o = _rms(o, g_ref[...]) * (1.0 - lam_init)
    o_ref[...] = o.astype(o_ref.dtype)


def diff_attention(qr, kr, z3, lam_vec, att_g, *, v_col_blk, q_row0, n_q, n_kv, lam_init, out, heads):
    bsz, t, att_w = qr.shape
    tq = _divisor(math.gcd(q_row0, n_q) if q_row0 else n_q, ROW_BLOCK, 8)
    tk = _divisor(n_kv, 1024, V7X_LANES)
    qoff = q_row0 // tq
    q_spec = pl.BlockSpec((None, tq, ATT_VDIM), lambda b, h, i: (b, i + qoff, h))
    return pl.pallas_call(
        functools.partial(_attn_kernel, tk=tk, lam_init=lam_init),
        grid=(bsz, heads, n_q // tq),
        in_specs=[pl.BlockSpec(lam_vec.shape, lambda b, h, i: (0, 0)),
                  pl.BlockSpec((1, ATT_VDIM), lambda b, h, i: (0, 0)),
                  q_spec,
                  pl.BlockSpec((None, n_kv, ATT_VDIM), lambda b, h, i: (b, 0, h)),
                  pl.BlockSpec((None, n_kv, ATT_VDIM), lambda b, h, i: (b, 0, v_col_blk + h)),
                  pl.BlockSpec(memory_space=pl.ANY)],
        out_specs=q_spec,
        out_shape=jax.ShapeDtypeStruct((bsz, t, att_w), BF16),
        input_output_aliases={5: 0},
        scratch_shapes=[pltpu.VMEM((tq, ATT_VDIM), F32), pltpu.VMEM((tq, ATT_VDIM), F32)],
        compiler_params=_params(("parallel", "parallel", "arbitrary")),
        name="diff_attention",
    )(lam_vec, att_g.reshape(1, ATT_VDIM), qr, kr, z3, out)


def _conv_kernel(cb_ref, cc_ref, ch_ref, w_ref, o_ref, p_ref, *, ctx_len, rc):
    t, tc = cb_ref.shape
    pad = 8
    p_ref[0:pad, :] = jnp.zeros((pad, tc), F32)
    p_ref[pad + t:pad + t + pad, :] = jnp.zeros((pad, tc), F32)
    for r0 in range(0, t, rc):
        p_ref[pad + r0:pad + r0 + rc, :] = cc_ref[r0:r0 + rc, :].astype(F32) * ch_ref[r0:r0 + rc, :].astype(F32)
    w0, w1, w2 = w_ref[0:1, :], w_ref[1:2, :], w_ref[2:3, :]
    for r0 in range(0, t, rc):
        row = r0 + lax.broadcasted_iota(jnp.int32, (rc, 1), 0)
        prev = p_ref[pad + r0 - 1:pad + r0 - 1 + rc, :]
        cur = p_ref[pad + r0:pad + r0 + rc, :]
        nxt = p_ref[pad + r0 + 1:pad + r0 + 1 + rc, :]
        prev = jnp.where(row == ctx_len, 0.0, prev)
        nxt = jnp.where(row == ctx_len - 1, 0.0, nxt)
        y = cb_ref[r0:r0 + rc, :].astype(F32) * (w0 * prev + w1 * cur + w2 * nxt)
        o_ref[r0:r0 + rc, :] = y.astype(o_ref.dtype)


def short_conv(z3, conv_w, *, cb_off, ctx_len):
    bsz, t, _ = z3.shape
    conv_wd = conv_w.shape[-1]
    tc = V7X_LANES
    rc = _divisor(t, 768, 8)
    blk0 = cb_off // tc
    nblk = conv_wd // tc
    spec = lambda k: pl.BlockSpec((None, t, tc), lambda b, j: (b, 0, blk0 + k * nblk + j))
    return pl.pallas_call(
        functools.partial(_conv_kernel, ctx_len=ctx_len, rc=rc),
        grid=(bsz, nblk),
        in_specs=[spec(0), spec(1), spec(2), pl.BlockSpec((3, tc), lambda b, j: (0, j))],
        out_specs=pl.BlockSpec((None, t, tc), lambda b, j: (b, 0, j)),
        out_shape=jax.ShapeDtypeStruct((bsz, t, conv_wd), BF16),
        scratch_shapes=[pltpu.VMEM((t + 16, tc), F32)],
        compiler_params=_params(("parallel", "parallel")),
        name="short_conv",
    )(z3, z3, z3, conv_w)


def ssm_matrices(a_re, a_im, log_dt, b_re, b_im, c_re, c_im, ssm_d, gp):
    tc = SSM_CHUNK
    n_g, n_p = a_re.shape[1], a_re.shape[2]
    n_i = b_re.shape[-1]
    dt = jnp.exp(log_dt.astype(F32))[..., None]
    lr = jnp.minimum(a_re.astype(F32), -1e-4)
    li = a_im.astype(F32)

    def power(n):
        nn = n.astype(F32)[:, None, None, None]
        mag = jnp.exp(nn * (lr * dt))
        return mag * jnp.cos(nn * (li * dt)), mag * jnp.sin(nn * (li * dt))

    ar, ai = power(jnp.ones((1,), F32))
    ar, ai = ar[0], ai[0]
    den = lr * lr + li * li
    cr = ((ar - 1.0) * lr + ai * li) / den
    ci = (ai * lr - (ar - 1.0) * li) / den
    bre, bim = b_re.astype(F32), b_im.astype(F32)
    br = cr[..., None] * bre - ci[..., None] * bim
    bi = cr[..., None] * bim + ci[..., None] * bre
    cre, cim = c_re.astype(F32), c_im.astype(F32)

    pr, pi = power(jnp.arange(tc + 1))
    wr = pr[..., None] * br - pi[..., None] * bi
    wi = pr[..., None] * bi + pi[..., None] * br
    kern = jnp.einsum('dgip,ndgpj->ndgij', cre, wr) - jnp.einsum('dgip,ndgpj->ndgij', cim, wi)

    r_idx = jnp.arange(tc)[:, None]
    s_idx = jnp.arange(tc)[None, :]

    def toeplitz(lag, d):
        blk = kern[jnp.clip(lag, 0, tc), d]
        blk = jnp.where((lag >= 0)[:, :, None, None, None], blk, 0.0)
        return blk.transpose(2, 0, 4, 1, 3).reshape(n_g, tc * n_i, tc * n_i)

    w = jnp.stack([toeplitz(s_idx - r_idx, 0), toeplitz(r_idx - s_idx, 1)])

    def lane_pad_cols(m):
        q = jnp.arange(n_g) % gp
        onehot = jax.nn.one_hot(q, gp, dtype=F32)
        return (m[:, :, None, :] * onehot[:, None, :, None]).reshape(n_g, m.shape[1], gp * n_p)

    def state_in(d, n_of_r):
        sel = n_of_r
        re = wr[sel, d].transpose(1, 0, 3, 2).reshape(n_g, tc * n_i, n_p)
        im = wi[sel, d].transpose(1, 0, 3, 2).reshape(n_g, tc * n_i, n_p)
        return jnp.stack([lane_pad_cols(re), lane_pad_cols(im)])

    we = jnp.stack([state_in(0, tc - 1 - jnp.arange(tc)), state_in(1, jnp.arange(tc))])

    def state_out(d, n_of_s):
        prs, pis = pr[n_of_s, d], pi[n_of_s, d]
        qr = cre[d][None] * prs[:, :, None, :] - cim[d][None] * pis[:, :, None, :]
        qi = cre[d][None] * pis[:, :, None, :] + cim[d][None] * prs[:, :, None, :]
        fre = qr.transpose(1, 0, 2, 3).reshape(n_g, tc * n_i, n_p)
        fim = -qi.transpose(1, 0, 2, 3).reshape(n_g, tc * n_i, n_p)
        return jnp.stack([lane_pad_cols(fre), lane_pad_cols(fim)]).transpose(0, 1, 3, 2)

    wf = jnp.stack([state_out(0, jnp.arange(tc) + 1), state_out(1, tc - jnp.arange(tc))])

    at = jnp.stack([pr[tc].reshape(2, n_g * n_p), pi[tc].reshape(2, n_g * n_p)], axis=1)
    dd = jnp.tile(ssm_d.astype(F32).reshape(n_g, 1, n_i), (1, 1, tc))
    return w.astype(BF16), we.astype(BF16), wf.astype(BF16), at, dd


def _ssm_state_kernel(u_ref, we_ref, sre_ref, sim_ref):
    gp = u_ref.shape[0]
    for d in range(2):
        for part, dst in ((0, sre_ref), (1, sim_ref)):
            acc = jnp.dot(u_ref[0], we_ref[d, part, 0], preferred_element_type=F32)
            for q in range(1, gp):
                acc = acc + jnp.dot(u_ref[q], we_ref[d, part, q], preferred_element_type=F32)
            dst[d] = acc


def _ssm_scan_kernel(sre_ref, sim_ref, at_ref, hre_ref, him_ref, *, bsz, nch, nctx):
    d = pl.program_id(0)
    atr, ati = at_ref[0], at_ref[1]
    zero = jnp.zeros(atr.shape, F32)

    def body(k, carry):
        c_rev = jnp.where(k < nctx, nctx - 1 - k, nch - 1 - (k - nctx))
        c = jnp.where(d == 0, k, c_rev)
        new = []
        for b in range(bsz):
            hr, hi = carry[2 * b], carry[2 * b + 1]
            row = b * nch + c
            hre_ref[row] = hr
            him_ref[row] = hi
            new.append(atr * hr - ati * hi + sre_ref[row])
            new.append(atr * hi + ati * hr + sim_ref[row])
        return tuple(new)

    lax.fori_loop(0, nch, body, tuple([zero] * (2 * bsz)))


def _ssm_out_kernel(u_ref, w_ref, wf_ref, hre_ref, him_ref, dd_ref, y_ref):
    gp = u_ref.shape[0]
    for q in range(gp):
        u = u_ref[q]
        y = u.astype(F32) * dd_ref[q]
        for d in range(2):
            y = y + jnp.dot(u, w_ref[d, q], preferred_element_type=F32)
            y = y + jnp.dot(hre_ref[d].astype(BF16), wf_ref[d, 0, q], preferred_element_type=F32)
            y = y + jnp.dot(him_ref[d].astype(BF16), wf_ref[d, 1, q], preferred_element_type=F32)
        y_ref[q] = y.astype(y_ref.dtype)


def ssm_branch(su, mats, *, ctx_len):
    w, we, wf, at, dd = mats
    bsz, t, width = su.shape
    tc = SSM_CHUNK
    n_g = w.shape[1]
    n_i = width // n_g
    gpp = we.shape[-1]
    n_p = at.shape[-1] // n_g
    gp = gpp // n_p
    nch = t // tc
    mc = bsz * nch
    ug = su.reshape(bsz, nch, tc, n_g, n_i).transpose(3, 0, 1, 2, 4).reshape(n_g, mc, tc * n_i)

    s_spec = pl.BlockSpec((2, mc, gpp), lambda g: (0, 0, g))
    s_shape = jax.ShapeDtypeStruct((2, mc, n_g * n_p), F32)
    sre, sim = pl.pallas_call(
        _ssm_state_kernel,
        grid=(n_g // gp,),
        in_specs=[pl.BlockSpec((gp, mc, tc * n_i), lambda g: (g, 0, 0)),
                  pl.BlockSpec((2, 2, gp, tc * n_i, gpp), lambda g: (0, 0, g, 0, 0))],
        out_specs=[s_spec, s_spec],
        out_shape=[s_shape, s_shape],
        compiler_params=_params(("parallel",)),
        name="ssm_chunk_state",
    )(ug, we)

    sub = 8
    lanes = math.gcd(n_g * n_p // sub, V7X_LANES)
    nlb = n_g * n_p // (sub * lanes)
    tile = lambda a: a.reshape(a.shape[:-1] + (nlb, sub, lanes))
    sre5, sim5, at5 = tile(sre), tile(sim), tile(at)
    st_spec = pl.BlockSpec((None, mc, None, sub, lanes), lambda dd_, j: (dd_, 0, j, 0, 0))
    h_shape = jax.ShapeDtypeStruct(sre5.shape, F32)
    hre, him = pl.pallas_call(
        functools.partial(_ssm_scan_kernel, bsz=bsz, nch=nch, nctx=ctx_len // tc),
        grid=(2, nlb),
        in_specs=[st_spec, st_spec, pl.BlockSpec((None, 2, None, sub, lanes), lambda dd_, j: (dd_, 0, j, 0, 0))],
        out_specs=[st_spec, st_spec],
        out_shape=[h_shape, h_shape],
        compiler_params=_params(("parallel", "parallel")),
        name="ssm_chunk_scan",
    )(sre5, sim5, at5)
    hre, him = hre.reshape(sre.shape), him.reshape(sim.shape)

    yg = pl.pallas_call(
        _ssm_out_kernel,
        grid=(n_g // gp,),
        in_specs=[pl.BlockSpec((gp, mc, tc * n_i), lambda g: (g, 0, 0)),
                  pl.BlockSpec((2, gp, tc * n_i, tc * n_i), lambda g: (0, g, 0, 0)),
                  pl.BlockSpec((2, 2, gp, gpp, tc * n_i), lambda g: (0, 0, g, 0, 0)),
                  s_spec, s_spec,
                  pl.BlockSpec((gp, 1, tc * n_i), lambda g: (g, 0, 0))],
        out_specs=pl.BlockSpec((gp, mc, tc * n_i), lambda g: (g, 0, 0)),
        out_shape=jax.ShapeDtypeStruct((n_g, mc, tc * n_i), BF16),
        compiler_params=_params(("parallel",)),
        name="ssm_chunk_out",
    )(ug, w, wf, hre, him, dd)
    return yg.reshape(n_g, bsz, nch, tc, n_i).transpose(1, 2, 3, 0, 4).reshape(bsz * t, width)


def _glu_kernel(y_ref, w_ref, b_ref, o_ref):
    y = y_ref[...].astype(F32)
    g = 0.5 * y * (1.0 + jnp.tanh(math.sqrt(2.0 / math.pi) * (y + 0.044715 * (y * y * y))))
    r = jnp.dot(g.astype(BF16), w_ref[...], preferred_element_type=F32) + b_ref[...]
    o_ref[...] = (g * jax.nn.sigmoid(r)).astype(o_ref.dtype)


def s5_glu(y, w, b):
    m, width = y.shape
    tm = _divisor(m, 768, 16)
    return pl.pallas_call(
        _glu_kernel,
        grid=(m // tm,),
        in_specs=[pl.BlockSpec((tm, width), lambda i: (i, 0)),
                  pl.BlockSpec((width, width), lambda i: (0, 0)),
                  pl.BlockSpec((1, width), lambda i: (0, 0))],
        out_specs=pl.BlockSpec((tm, width), lambda i: (i, 0)),
        out_shape=jax.ShapeDtypeStruct((m, width), BF16),
        compiler_params=_params(("parallel",)),
        name="s5_glu",
    )(y, w, b.reshape(1, width))


def _merge_kernel(att_ref, conv_ref, ssm_ref, g0_ref, g1_ref, g2_ref, pa_ref, pc_ref, ps_ref, o_ref):
    def branch(x_ref, p_ref, g_ref):
        y = jnp.dot(x_ref[...], p_ref[...], preferred_element_type=F32)
        return jax.nn.sigmoid(g_ref[...].astype(F32)) * y

    m = branch(att_ref, pa_ref, g0_ref) + branch(conv_ref, pc_ref, g1_ref) + branch(ssm_ref, ps_ref, g2_ref)
    o_ref[...] = m.astype(o_ref.dtype)


def branch_merge(att, conv, ssm, z, p_att, p_conv, p_ssm, *, g_off):
    m, d = att.shape[0], p_att.shape[1]
    tm = _divisor(m, 768, 16)
    tn = _divisor(math.gcd(d, g_off), 1024, V7X_LANES)
    gblk = g_off // tn
    nj = d // tn
    row = lambda a: pl.BlockSpec((tm, a.shape[1]), lambda i, j: (i, 0))
    gate = lambda k: pl.BlockSpec((tm, tn), lambda i, j: (i, gblk + k * nj + j))
    col = lambda p: pl.BlockSpec((p.shape[0], tn), lambda i, j: (0, j))
    return pl.pallas_call(
        _merge_kernel,
        grid=(m // tm, nj),
        in_specs=[row(att), row(conv), row(ssm), gate(0), gate(1), gate(2), col(p_att), col(p_conv), col(p_ssm)],
        out_specs=pl.BlockSpec((tm, tn), lambda i, j: (i, j)),
        out_shape=jax.ShapeDtypeStruct((m, d), BF16),
        compiler_params=_params(("parallel", "parallel")),
        name="branch_merge",
    )(att, conv, ssm, z, z, z, p_att, p_conv, p_ssm)


def _first_max(x, axis, iota):
    mx = jnp.max(x, axis=axis, keepdims=True)
    n = x.shape[axis]
    idx = jnp.min(jnp.where(x == mx, iota, n), axis=axis, keepdims=True)
    return mx, idx, iota == idx


def _router_kernel(lg_ref, b_ref, gate_ref, idx_ref, wgt_ref):
    n_e, tm = lg_ref.shape
    per = n_e // N_EXPERT_GROUPS
    scores = jax.nn.sigmoid(lg_ref[...])
    sel = scores + b_ref[...]
    neg = -jnp.inf
    grp = sel.reshape(N_EXPERT_GROUPS, per, tm)
    iota_e = lax.broadcasted_iota(jnp.int32, grp.shape, 1)
    m1, _, first = _first_max(grp, 1, iota_e)
    m2 = jnp.max(jnp.where(first, neg, grp), axis=1, keepdims=True)
    gscore = m1 + m2
    iota_g = lax.broadcasted_iota(jnp.int32, gscore.shape, 0)
    gmask = jnp.zeros(gscore.shape, F32)
    for _ in range(TOPK_GROUPS):
        _, _, hit = _first_max(gscore, 0, iota_g)
        gmask = jnp.where(hit, 1.0, gmask)
        gscore = jnp.where(hit, neg, gscore)
    cand = jnp.where(jnp.broadcast_to(gmask, grp.shape) > 0.0, grp, neg).reshape(n_e, tm)
    iota_x = lax.broadcasted_iota(jnp.int32, cand.shape, 0)
    chosen = jnp.zeros(cand.shape, F32)
    for r in range(TOP_K):
        _, idx, hit = _first_max(cand, 0, iota_x)
        chosen = jnp.where(hit, 1.0, chosen)
        cand = jnp.where(hit, neg, cand)
        idx_ref[r:r + 1, :] = idx
        wgt_ref[r:r + 1, :] = jnp.sum(jnp.where(hit, scores, 0.0), axis=0, keepdims=True)
    picked = chosen * scores
    norm = ROUTED_SCALE / jnp.sum(picked, axis=0, keepdims=True)
    wgt_ref[...] = wgt_ref[...] * norm
    gate_ref[...] = picked * norm


def route(logits_t, router_b):
    n_e, m = logits_t.shape
    tm = _divisor(m, 512, V7X_LANES)
    spec = lambda r: pl.BlockSpec((r, tm), lambda i: (0, i))
    return pl.pallas_call(
        _router_kernel,
        grid=(m // tm,),
        in_specs=[spec(n_e), pl.BlockSpec((n_e, 1), lambda i: (0, 0))],
        out_specs=[spec(n_e), spec(TOP_K), spec(TOP_K)],
        out_shape=[jax.ShapeDtypeStruct((n_e, m), F32), jax.ShapeDtypeStruct((TOP_K, m), jnp.int32),
                   jax.ShapeDtypeStruct((TOP_K, m), F32)],
        compiler_params=_params(("parallel",)),
        name="moe_route",
    )(logits_t, router_b.reshape(n_e, 1))


def _expert_up_kernel(x_ref, wg_ref, wu_ref, gate_ref, o_ref):
    x = x_ref[...]
    g = jnp.dot(x, wg_ref[...], preferred_element_type=F32)
    u = jnp.dot(x, wu_ref[...], preferred_element_type=F32)
    e = pl.program_id(1)
    lane = lax.broadcasted_iota(jnp.int32, gate_ref.shape, 1)
    gate = jnp.sum(jnp.where(lane == e, gate_ref[...], 0.0), axis=-1, keepdims=True)
    o_ref[...] = (g * jax.nn.sigmoid(g) * u * gate).astype(o_ref.dtype)


def expert_up(h, w_gate, w_up, gate):
    m, d = h.shape
    n_e, _, ff = w_gate.shape
    tm = _divisor(m, 768, 16)
    wspec = pl.BlockSpec((None, d, ff), lambda i, e: (e, 0, 0))
    return pl.pallas_call(
        _expert_up_kernel,
        grid=(m // tm, n_e),
        in_specs=[pl.BlockSpec((tm, d), lambda i, e: (i, 0)), wspec, wspec,
                  pl.BlockSpec((tm, gate.shape[1]), lambda i, e: (i, 0))],
        out_specs=pl.BlockSpec((tm, ff), lambda i, e: (i, e)),
        out_shape=jax.ShapeDtypeStruct((m, n_e * ff), BF16),
        compiler_params=_params(("parallel", "arbitrary")),
        name="expert_up",
    )(h, w_gate, w_up, gate)


def _layer_weights(i, w_in, p_att, p_conv, p_ssm, w_o, glu_w, router_w, exp_w_gate, exp_w_up, exp_w_down,
                   sh_w_gate, sh_w_up, sh_w_down):
    rw_t = router_w[i].T
    rw_hi = rw_t.astype(BF16)
    rw_lo = (rw_t - rw_hi.astype(F32)).astype(BF16)
    wg = jnp.concatenate([exp_w_gate[i], sh_w_gate[i][None]], axis=0).astype(BF16)
    wu = jnp.concatenate([exp_w_up[i], sh_w_up[i][None]], axis=0).astype(BF16)
    wd = jnp.concatenate([exp_w_down[i].reshape(-1, exp_w_down.shape[-1]), sh_w_down[i]], axis=0).astype(BF16)
    return dict(w_in=w_in[i].astype(BF16), p_att=p_att[i].astype(BF16), p_conv=p_conv[i].astype(BF16),
                p_ssm=p_ssm[i].astype(BF16), w_o=w_o[i].astype(BF16), glu_w=glu_w[i].astype(BF16),
                router=(rw_hi, rw_lo), wg=wg, wu=wu, wd=wd)


def kernel(x, c, ctx, c_ctx, ada_w, ada_b, norm_g, w_in, lam_vec, att_g, conv_w,
           ssm_a_re, ssm_a_im, ssm_log_dt, ssm_b_re, ssm_b_im, ssm_c_re, ssm_c_im, ssm_d,
           glu_w, glu_b, p_att, p_conv, p_ssm, w_o, router_w, router_b,
           exp_w_gate, exp_w_up, exp_w_down, sh_w_gate, sh_w_up, sh_w_down):
    bsz, n_lat, d = x.shape
    n_ctx = ctx.shape[1]
    t = n_ctx + n_lat
    m = bsz * t
    depth = ada_w.shape[0]
    att_w = p_att.shape[1]
    heads = att_w // ATT_VDIM
    conv_wd = conv_w.shape[-1]
    ssm_wd = ssm_d.shape[-1]
    n_e = router_w.shape[-1]
    n_p = ssm_a_re.shape[-1]
    assert bsz + 1 <= 8 and n_ctx % SSM_CHUNK == 0 and n_lat % SSM_CHUNK == 0 and n_lat % GRID_W == 0
    assert V7X_LANES % n_p == 0 and n_e % N_EXPERT_GROUPS == 0
    k_off = att_w
    v_off = 2 * att_w
    cb_off = 3 * att_w
    su_off = cb_off + 3 * conv_wd
    g_off = su_off + ssm_wd

    stream = jnp.concatenate([ctx, x], axis=1)
    cond_cols = jnp.zeros((d, 8), F32).at[:, :bsz].set(c.T).at[:, bsz].set(c_ctx)
    tables = rope_tables(n_ctx, n_lat)
    gate_pad = V7X_LANES * pl.cdiv(n_e + 1, V7X_LANES)

    pending = None
    for i in range(depth):
        lw = _layer_weights(i, w_in, p_att, p_conv, p_ssm, w_o, glu_w, router_w, exp_w_gate, exp_w_up,
                            exp_w_down, sh_w_gate, sh_w_up, sh_w_down)
        lam_init = 0.8 - 0.6 * math.exp(-0.3 * i)
        mod8 = modulation(cond_cols, ada_w[i], ada_b[i], bsz + 1).reshape(8, N_MOD, d)
        mod = jnp.stack([jnp.broadcast_to(mod8[bsz], (bsz, N_MOD, d)), mod8[:bsz]], axis=1)
        gains = norm_g[i]

        if pending is None:
            (h,) = resid_adaln(stream, None, mod, gains, ctx_len=n_ctx, pre_idx=0, shift_idx=0, scale_idx=1)
        else:
            branch, p_mod, p_gains = pending
            (stream,) = resid_adaln(stream, branch, p_mod, p_gains, ctx_len=n_ctx, post_idx=3, gate_idx=5)
            (h,) = resid_adaln(stream, None, mod, gains, ctx_len=n_ctx, pre_idx=0, shift_idx=0, scale_idx=1)
        z = matmul(h.reshape(m, d), lw['w_in'])
        z3 = z.reshape(bsz, t, -1)

        qr, kr = rope_qk(z3, tables, att_w, n_ctx)
        att = jnp.zeros((bsz, t, att_w), BF16)
        att = diff_attention(qr, kr, z3, lam_vec[i], att_g[i], v_col_blk=v_off // ATT_VDIM, q_row0=0, n_q=n_ctx,
                             n_kv=n_ctx, lam_init=lam_init, out=att, heads=heads)
        att = diff_attention(qr, kr, z3, lam_vec[i], att_g[i], v_col_blk=v_off // ATT_VDIM, q_row0=n_ctx, n_q=n_lat,
                             n_kv=t, lam_init=lam_init, out=att, heads=heads)

        conv = short_conv(z3, conv_w[i], cb_off=cb_off, ctx_len=n_ctx)

        mats = ssm_matrices(ssm_a_re[i], ssm_a_im[i], ssm_log_dt[i], ssm_b_re[i], ssm_b_im[i],
                            ssm_c_re[i], ssm_c_im[i], ssm_d[i], V7X_LANES // n_p)
        y = ssm_branch(z3[:, :, su_off:su_off + ssm_wd], mats, ctx_len=n_ctx)
        ssm = s5_glu(y, lw['glu_w'], glu_b[i])

        merged = branch_merge(att.reshape(m, att_w), conv.reshape(m, conv_wd), ssm, z,
                              lw['p_att'], lw['p_conv'], lw['p_ssm'], g_off=g_off)
        o = matmul(merged, lw['w_o']).reshape(bsz, t, d)

        stream, h2, logits_t = resid_adaln(stream, o, mod, gains, ctx_len=n_ctx, post_idx=1, gate_idx=2,
                                           pre_idx=2, shift_idx=3, scale_idx=4, router=lw['router'])
        gate_t, _, _ = route(logits_t, router_b[i])
        gate = jnp.zeros((m, gate_pad), F32).at[:, :n_e].set(gate_t.T).at[:, n_e].set(1.0)
        hid = expert_up(h2.reshape(m, d), lw['wg'], lw['wu'], gate)
        f = matmul(hid, lw['wd'], tk_target=2048).reshape(bsz, t, d)
        pending = (f, mod, gains)

    branch, p_mod, p_gains = pending
    (out,) = resid_adaln(stream, branch, p_mod, p_gains, ctx_len=n_ctx, post_idx=3, gate_idx=5,
                         latent_only_out=True)
    return out
```

```python
import functools
import math

import jax
import jax.numpy as jnp
from jax import lax
from jax.experimental import pallas as pl
from jax.experimental.pallas import tpu as pltpu

F32 = jnp.float32
BF16 = jnp.bfloat16

GRID_W = 64
ROPE_BASE = 10000.0
ATT_DIM = 128
ATT_VDIM = 2 * ATT_DIM
N_BRANCH = 3
TOP_K = 8
N_EXPERT_GROUPS = 8
TOPK_GROUPS = 4
ROUTED_SCALE = 2.5
N_MOD = 6
EPS = 1e-6
SSM_CHUNK = 16

V7X_LANES = 128
V7X_VMEM_LIMIT_BYTES = 56 * 1024 * 1024
ROW_BLOCK = 256


def _divisor(n, target, mult):
    best = None
    for d in range(mult, min(n, target) + 1, mult):
        if n % d == 0:
            best = d
    return best if best is not None else n


def _params(sem):
    return pltpu.CompilerParams(dimension_semantics=sem, vmem_limit_bytes=V7X_VMEM_LIMIT_BYTES)


def _mod_kernel(s_ref, w_ref, b_ref, o_ref, *, rows, kc):
    s = s_ref[...]
    s = s * jax.nn.sigmoid(s)
    o_ref[...] = jnp.zeros(o_ref.shape, F32)
    d = w_ref.shape[0]
    for r in range(rows):
        acc = b_ref[...]
        for k0 in range(0, d, kc):
            acc = acc + jnp.sum(w_ref[k0:k0 + kc, :] * s[k0:k0 + kc, r:r + 1], axis=0, keepdims=True)
        o_ref[r:r + 1, :] = acc


def modulation(cond_cols, w, b, rows):
    d, n = w.shape
    tn = _divisor(n, 512, V7X_LANES)
    kc = _divisor(d, 512, 8)
    return pl.pallas_call(
        functools.partial(_mod_kernel, rows=rows, kc=kc),
        grid=(n // tn,),
        in_specs=[pl.BlockSpec((d, 8), lambda j: (0, 0)),
                  pl.BlockSpec((d, tn), lambda j: (0, j)),
                  pl.BlockSpec((1, tn), lambda j: (0, j))],
        out_specs=pl.BlockSpec((8, tn), lambda j: (0, j)),
        out_shape=jax.ShapeDtypeStruct((8, n), F32),
        compiler_params=_params(("arbitrary",)),
        name="modulation",
    )(cond_cols, w, b.reshape(1, n))


def _rms(x, g):
    return x * lax.rsqrt(jnp.mean(x * x, axis=-1, keepdims=True) + EPS) * g


def _pack_halves(a, b):
    lo = lax.bitcast_convert_type(a.astype(BF16).astype(F32), jnp.uint32) >> 16
    hi = lax.bitcast_convert_type(b.astype(BF16).astype(F32), jnp.uint32) & jnp.uint32(0xFFFF0000)
    return lo | hi


def _unpack_halves(p):
    lo = lax.bitcast_convert_type(p << 16, F32)
    hi = lax.bitcast_convert_type(p & jnp.uint32(0xFFFF0000), F32)
    return lo, hi


def _resid_adaln_kernel(*refs, has_branch, has_h, has_router, post_idx, gate_idx, pre_idx, shift_idx, scale_idx):
    it = iter(refs)
    x_ref = next(it)
    o_ref = next(it) if has_branch else None
    mod_ref = next(it)
    g_ref = next(it)
    rwh_ref = next(it) if has_router else None
    rwl_ref = next(it) if has_router else None
    xout_ref = next(it) if has_branch else None
    h_ref = next(it) if has_h else None
    hp_ref = next(it) if has_router else None
    lg_ref = next(it) if has_router else None

    x = x_ref[...]
    if has_branch:
        o = o_ref[...].astype(F32)
        x = x + mod_ref[gate_idx:gate_idx + 1, :] * _rms(o, g_ref[post_idx:post_idx + 1, :])
        xout_ref[...] = x
    if has_h:
        h = _rms(x, g_ref[pre_idx:pre_idx + 1, :])
        h = h * (1.0 + mod_ref[scale_idx:scale_idx + 1, :]) + mod_ref[shift_idx:shift_idx + 1, :]
        h_ref[...] = h.astype(BF16)
        if has_router:
            h_hi = h.astype(BF16)
            h_lo = (h - h_hi.astype(F32)).astype(BF16)
            nt = (((1,), (1,)), ((), ()))
            lg = lax.dot_general(rwh_ref[...], h_hi, nt, preferred_element_type=F32)
            lg = lg + lax.dot_general(rwh_ref[...], h_lo, nt, preferred_element_type=F32)
            lg = lg + lax.dot_general(rwl_ref[...], h_hi, nt, preferred_element_type=F32)
            lg_ref[...] = lg
            half = h.shape[1] // 2
            hp_ref[...] = _pack_halves(h[:, :half], h[:, half:])


def resid_adaln(x, branch, mod, gains, *, ctx_len, post_idx=0, gate_idx=0, pre_idx=None,
                shift_idx=0, scale_idx=0, router=None, latent_only_out=False):
    bsz, t, d = x.shape
    rb = _divisor(math.gcd(ctx_len, t - ctx_len), ROW_BLOCK, 8)
    nctx = ctx_len // rb
    has_branch = branch is not None
    has_h = pre_idx is not None
    has_router = router is not None
    off = nctx if latent_only_out else 0
    nblk = t // rb - off

    def row_map(b, i):
        return (b, i + off, 0)

    in_specs = [pl.BlockSpec((None, rb, d), row_map)]
    args = [x]
    if has_branch:
        in_specs.append(pl.BlockSpec((None, rb, d), row_map))
        args.append(branch)
    in_specs.append(pl.BlockSpec((None, None, mod.shape[2], d),
                                 lambda b, i: (b, jnp.where(i + off >= nctx, 1, 0), 0, 0)))
    args.append(mod)
    in_specs.append(pl.BlockSpec(gains.shape, lambda b, i: (0, 0)))
    args.append(gains)
    if has_router:
        rw_hi, rw_lo = router
        in_specs += [pl.BlockSpec(rw_hi.shape, lambda b, i: (0, 0)), pl.BlockSpec(rw_lo.shape, lambda b, i: (0, 0))]
        args += [rw_hi, rw_lo]
    out_specs, out_shape = [], []
    if has_branch:
        out_specs.append(pl.BlockSpec((None, rb, d), lambda b, i: (b, i, 0)))
        out_shape.append(jax.ShapeDtypeStruct((bsz, nblk * rb, d), F32))
    if has_h:
        out_specs.append(pl.BlockSpec((None, rb, d), lambda b, i: (b, i, 0)))
        out_shape.append(jax.ShapeDtypeStruct((bsz, t, d), BF16))
    if has_router:
        n_e = router[0].shape[0]
        out_specs.append(pl.BlockSpec((None, rb, d // 2), lambda b, i: (b, i, 0)))
        out_shape.append(jax.ShapeDtypeStruct((bsz, t, d // 2), jnp.uint32))
        out_specs.append(pl.BlockSpec((n_e, rb), lambda b, i: (0, b * nblk + i)))
        out_shape.append(jax.ShapeDtypeStruct((n_e, bsz * t), F32))
    return pl.pallas_call(
        functools.partial(_resid_adaln_kernel, has_branch=has_branch, has_h=has_h, has_router=has_router,
                          post_idx=post_idx, gate_idx=gate_idx, pre_idx=pre_idx, shift_idx=shift_idx,
                          scale_idx=scale_idx),
        grid=(bsz, nblk),
        in_specs=in_specs,
        out_specs=out_specs,
        out_shape=out_shape,
        compiler_params=_params(("parallel", "parallel")),
        name="resid_adaln",
    )(*args)


def _mm_kernel(a_ref, b_ref, o_ref):
    o_ref[...] = jnp.dot(a_ref[...], b_ref[...], preferred_element_type=F32).astype(o_ref.dtype)


def _mm_acc_kernel(a_ref, b_ref, o_ref, acc_ref):
    @pl.when(pl.program_id(2) == 0)
    def _():
        acc_ref[...] = jnp.zeros(acc_ref.shape, F32)

    acc_ref[...] += jnp.dot(a_ref[...], b_ref[...], preferred_element_type=F32)

    @pl.when(pl.program_id(2) == pl.num_programs(2) - 1)
    def _():
        o_ref[...] = acc_ref[...].astype(o_ref.dtype)


def matmul(a, b, out_dtype=BF16, tm_target=768, tn_target=1024, tk_target=None):
    m, k = a.shape
    _, n = b.shape
    tm = _divisor(m, tm_target, 16)
    tn = _divisor(n, tn_target, V7X_LANES)
    if tk_target is None or tk_target >= k:
        return pl.pallas_call(
            _mm_kernel,
            grid=(m // tm, n // tn),
            in_specs=[pl.BlockSpec((tm, k), lambda i, j: (i, 0)), pl.BlockSpec((k, tn), lambda i, j: (0, j))],
            out_specs=pl.BlockSpec((tm, tn), lambda i, j: (i, j)),
            out_shape=jax.ShapeDtypeStruct((m, n), out_dtype),
            compiler_params=_params(("parallel", "parallel")),
            name="matmul",
        )(a, b)
    tk = _divisor(k, tk_target, V7X_LANES)
    return pl.pallas_call(
        _mm_acc_kernel,
        grid=(m // tm, n // tn, k // tk),
        in_specs=[pl.BlockSpec((tm, tk), lambda i, j, l: (i, l)), pl.BlockSpec((tk, tn), lambda i, j, l: (l, j))],
        out_specs=pl.BlockSpec((tm, tn), lambda i, j, l: (i, j)),
        out_shape=jax.ShapeDtypeStruct((m, n), out_dtype),
        scratch_shapes=[pltpu.VMEM((tm, tn), F32)],
        compiler_params=_params(("parallel", "parallel", "arbitrary")),
        name="matmul_ksplit",
    )(a, b)


def rope_tables(ctx_len, n_lat):
    nf = ATT_DIM // 4
    rows = n_lat // GRID_W
    row = jnp.repeat(jnp.arange(rows), GRID_W).astype(F32)
    col = jnp.tile(jnp.arange(GRID_W), rows).astype(F32)
    inv = ROPE_BASE ** (-jnp.arange(nf, dtype=F32) / nf)
    ang_r = row[:, None] * inv
    ang_c = col[:, None] * inv
    zero = jnp.zeros_like(ang_r)
    cos = jnp.concatenate([jnp.cos(ang_r), jnp.cos(ang_r), jnp.cos(ang_c), jnp.cos(ang_c)], axis=-1)
    sa = jnp.concatenate([-jnp.sin(ang_r), zero, -jnp.sin(ang_c), zero], axis=-1)
    sb = jnp.concatenate([zero, jnp.sin(ang_r), zero, jnp.sin(ang_c)], axis=-1)
    pad = lambda tbl, v: jnp.concatenate([jnp.full((ctx_len, ATT_DIM), v, F32), tbl], axis=0)
    return pad(cos, 1.0), pad(sa, 0.0), pad(sb, 0.0)


def _rope_kernel(q_ref, k_ref, cos_ref, sa_ref, sb_ref, qo_ref, ko_ref, *, q_scale):
    cos, sa, sb = cos_ref[...], sa_ref[...], sb_ref[...]
    quarter = ATT_DIM // 4
    for src, dst, scale in ((q_ref, qo_ref, q_scale), (k_ref, ko_ref, 1.0)):
        for g0 in range(0, src.shape[-1], ATT_DIM):
            x = src[:, g0:g0 + ATT_DIM].astype(F32)
            y = x * cos + pltpu.roll(x, ATT_DIM - quarter, 1) * sa + pltpu.roll(x, quarter, 1) * sb
            dst[:, g0:g0 + ATT_DIM] = (y * scale).astype(dst.dtype)


def rope_qk(z3, tables, att_w, ctx_len):
    bsz, t, _ = z3.shape
    rb = _divisor(math.gcd(ctx_len, t - ctx_len), ROW_BLOCK, 8)
    tbl_spec = pl.BlockSpec((rb, ATT_DIM), lambda b, i: (i, 0))
    out_spec = pl.BlockSpec((None, rb, att_w), lambda b, i: (b, i, 0))
    return pl.pallas_call(
        functools.partial(_rope_kernel, q_scale=ATT_DIM ** -0.5),
        grid=(bsz, t // rb),
        in_specs=[pl.BlockSpec((None, rb, att_w), lambda b, i: (b, i, 0)),
                  pl.BlockSpec((None, rb, att_w), lambda b, i: (b, i, 1)),
                  tbl_spec, tbl_spec, tbl_spec],
        out_specs=[out_spec, out_spec],
        out_shape=[jax.ShapeDtypeStruct((bsz, t, att_w), BF16)] * 2,
        compiler_params=_params(("parallel", "parallel")),
        name="rope_qk",
    )(z3, z3, *tables)


def _attn_kernel(lam_ref, g_ref, q_ref, k_ref, v_ref, prev_ref, o_ref, acc1_ref, acc2_ref, *, tk, lam_init):
    del prev_ref
    q = q_ref[...]
    q1, q2 = q[:, :ATT_DIM], q[:, ATT_DIM:]
    lv = lam_ref[...]
    lam = (jnp.exp(jnp.sum(lv[0:1] * lv[1:2], axis=-1, keepdims=True))
           - jnp.exp(jnp.sum(lv[2:3] * lv[3:4], axis=-1, keepdims=True)) + lam_init)
    tq = q.shape[0]
    nt = (((1,), (1,)), ((), ()))
    acc1_ref[...] = jnp.zeros(acc1_ref.shape, F32)
    acc2_ref[...] = jnp.zeros(acc2_ref.shape, F32)

    def one_map(qm, km, vc, acc_ref, m, l):
        s = lax.dot_general(qm, km, nt, preferred_element_type=F32)
        m_new = jnp.maximum(m, jnp.max(s, axis=-1, keepdims=True))
        alpha = jnp.exp(m - m_new)
        p = jnp.exp(s - m_new)
        l_new = alpha * l + jnp.sum(p, axis=-1, keepdims=True)
        acc_ref[...] = alpha * acc_ref[...] + jnp.dot(p.astype(BF16), vc, preferred_element_type=F32)
        return m_new, l_new

    def body(c, carry):
        m1, l1, m2, l2 = carry
        r0 = pl.multiple_of(c * tk, tk)
        kc = k_ref[pl.ds(r0, tk), :]
        vc = v_ref[pl.ds(r0, tk), :]
        m1, l1 = one_map(q1, kc[:, :ATT_DIM], vc, acc1_ref, m1, l1)
        m2, l2 = one_map(q2, kc[:, ATT_DIM:], vc, acc2_ref, m2, l2)
        return m1, l1, m2, l2

    neg = jnp.full((tq, 1), -jnp.inf, F32)
    zero = jnp.zeros((tq, 1), F32)
    _, l1, _, l2 = lax.fori_loop(0, k_ref.shape[0] // tk, body, (neg, zero, neg, zero))
    o = acc1_ref[...] / l1 - lam * (acc2_ref[...] / l2)
    o = _rms(o, g_ref[...]) * (1.0 - lam_init)
    o_ref[...] = o.astype(o_ref.dtype)


def diff_attention(qr, kr, z3, lam_vec, att_g, *, v_col_blk, q_row0, n_q, n_kv, lam_init, out, heads):
    bsz, t, att_w = qr.shape
    tq = _divisor(math.gcd(q_row0, n_q) if q_row0 else n_q, ROW_BLOCK, 8)
    tk = _divisor(n_kv, 1024, V7X_LANES)
    qoff = q_row0 // tq
    q_spec = pl.BlockSpec((None, tq, ATT_VDIM), lambda b, h, i: (b, i + qoff, h))
    return pl.pallas_call(
        functools.partial(_attn_kernel, tk=tk, lam_init=lam_init),
        grid=(bsz, heads, n_q // tq),
        in_specs=[pl.BlockSpec(lam_vec.shape, lambda b, h, i: (0, 0)),
                  pl.BlockSpec((1, ATT_VDIM), lambda b, h, i: (0, 0)),
                  q_spec,
                  pl.BlockSpec((None, n_kv, ATT_VDIM), lambda b, h, i: (b, 0, h)),
                  pl.BlockSpec((None, n_kv, ATT_VDIM), lambda b, h, i: (b, 0, v_col_blk + h)),
                  pl.BlockSpec(memory_space=pl.ANY)],
        out_specs=q_spec,
        out_shape=jax.ShapeDtypeStruct((bsz, t, att_w), BF16),
        input_output_aliases={5: 0},
        scratch_shapes=[pltpu.VMEM((tq, ATT_VDIM), F32), pltpu.VMEM((tq, ATT_VDIM), F32)],
        compiler_params=_params(("parallel", "parallel", "arbitrary")),
        name="diff_attention",
    )(lam_vec, att_g.reshape(1, ATT_VDIM), qr, kr, z3, out)


def _conv_kernel(cb_ref, cc_ref, ch_ref, w_ref, o_ref, p_ref, *, ctx_len, rc):
    t, tc = cb_ref.shape
    pad = 8
    p_ref[0:pad, :] = jnp.zeros((pad, tc), F32)
    p_ref[pad + t:pad + t + pad, :] = jnp.zeros((pad, tc), F32)
    for r0 in range(0, t, rc):
        p_ref[pad + r0:pad + r0 + rc, :] = cc_ref[r0:r0 + rc, :].astype(F32) * ch_ref[r0:r0 + rc, :].astype(F32)
    w0, w1, w2 = w_ref[0:1, :], w_ref[1:2, :], w_ref[2:3, :]
    for r0 in range(0, t, rc):
        row = r0 + lax.broadcasted_iota(jnp.int32, (rc, 1), 0)
        prev = p_ref[pad + r0 - 1:pad + r0 - 1 + rc, :]
        cur = p_ref[pad + r0:pad + r0 + rc, :]
        nxt = p_ref[pad + r0 + 1:pad + r0 + 1 + rc, :]
        prev = jnp.where(row == ctx_len, 0.0, prev)
        nxt = jnp.where(row == ctx_len - 1, 0.0, nxt)
        y = cb_ref[r0:r0 + rc, :].astype(F32) * (w0 * prev + w1 * cur + w2 * nxt)
        o_ref[r0:r0 + rc, :] = y.astype(o_ref.dtype)


def short_conv(z3, conv_w, *, cb_off, ctx_len):
    bsz, t, _ = z3.shape
    conv_wd = conv_w.shape[-1]
    tc = V7X_LANES
    rc = _divisor(t, 768, 8)
    blk0 = cb_off // tc
    nblk = conv_wd // tc
    spec = lambda k: pl.BlockSpec((None, t, tc), lambda b, j: (b, 0, blk0 + k * nblk + j))
    return pl.pallas_call(
        functools.partial(_conv_kernel, ctx_len=ctx_len, rc=rc),
        grid=(bsz, nblk),
        in_specs=[spec(0), spec(1), spec(2), pl.BlockSpec((3, tc), lambda b, j: (0, j))],
        out_specs=pl.BlockSpec((None, t, tc), lambda b, j: (b, 0, j)),
        out_shape=jax.ShapeDtypeStruct((bsz, t, conv_wd), BF16),
        scratch_shapes=[pltpu.VMEM((t + 16, tc), F32)],
        compiler_params=_params(("parallel", "parallel")),
        name="short_conv",
    )(z3, z3, z3, conv_w)


def ssm_matrices(a_re, a_im, log_dt, b_re, b_im, c_re, c_im, ssm_d, gp):
    tc = SSM_CHUNK
    n_g, n_p = a_re.shape[1], a_re.shape[2]
    n_i = b_re.shape[-1]
    dt = jnp.exp(log_dt.astype(F32))[..., None]
    lr = jnp.minimum(a_re.astype(F32), -1e-4)
    li = a_im.astype(F32)

    def power(n):
        nn = n.astype(F32)[:, None, None, None]
        mag = jnp.exp(nn * (lr * dt))
        return mag * jnp.cos(nn * (li * dt)), mag * jnp.sin(nn * (li * dt))

    ar, ai = power(jnp.ones((1,), F32))
    ar, ai = ar[0], ai[0]
    den = lr * lr + li * li
    cr = ((ar - 1.0) * lr + ai * li) / den
    ci = (ai * lr - (ar - 1.0) * li) / den
    bre, bim = b_re.astype(F32), b_im.astype(F32)
    br = cr[..., None] * bre - ci[..., None] * bim
    bi = cr[..., None] * bim + ci[..., None] * bre
    cre, cim = c_re.astype(F32), c_im.astype(F32)

    pr, pi = power(jnp.arange(tc + 1))
    wr = pr[..., None] * br - pi[..., None] * bi
    wi = pr[..., None] * bi + pi[..., None] * br
    kern = jnp.einsum('dgip,ndgpj->ndgij', cre, wr) - jnp.einsum('dgip,ndgpj->ndgij', cim, wi)

    r_idx = jnp.arange(tc)[:, None]
    s_idx = jnp.arange(tc)[None, :]

    def toeplitz(lag, d):
        blk = kern[jnp.clip(lag, 0, tc), d]
        blk = jnp.where((lag >= 0)[:, :, None, None, None], blk, 0.0)
        return blk.transpose(2, 0, 4, 1, 3).reshape(n_g, tc * n_i, tc * n_i)

    w = jnp.stack([toeplitz(s_idx - r_idx, 0), toeplitz(r_idx - s_idx, 1)])

    def lane_pad_cols(m):
        q = jnp.arange(n_g) % gp
        onehot = jax.nn.one_hot(q, gp, dtype=F32)
        return (m[:, :, None, :] * onehot[:, None, :, None]).reshape(n_g, m.shape[1], gp * n_p)

    def state_in(d, n_of_r):
        sel = n_of_r
        re = wr[sel, d].transpose(1, 0, 3, 2).reshape(n_g, tc * n_i, n_p)
        im = wi[sel, d].transpose(1, 0, 3, 2).reshape(n_g, tc * n_i, n_p)
        return jnp.stack([lane_pad_cols(re), lane_pad_cols(im)])

    we = jnp.stack([state_in(0, tc - 1 - jnp.arange(tc)), state_in(1, jnp.arange(tc))])

    def state_out(d, n_of_s):
        prs, pis = pr[n_of_s, d], pi[n_of_s, d]
        qr = cre[d][None] * prs[:, :, None, :] - cim[d][None] * pis[:, :, None, :]
        qi = cre[d][None] * pis[:, :, None, :] + cim[d][None] * prs[:, :, None, :]
        fre = qr.transpose(1, 0, 2, 3).reshape(n_g, tc * n_i, n_p)
        fim = -qi.transpose(1, 0, 2, 3).reshape(n_g, tc * n_i, n_p)
        return jnp.stack([lane_pad_cols(fre), lane_pad_cols(fim)]).transpose(0, 1, 3, 2)

    wf = jnp.stack([state_out(0, jnp.arange(tc) + 1), state_out(1, tc - jnp.arange(tc))])

    at = jnp.stack([pr[tc].reshape(2, n_g * n_p), pi[tc].reshape(2, n_g * n_p)], axis=1)
    dd = jnp.tile(ssm_d.astype(F32).reshape(n_g, 1, n_i), (1, 1, tc))
    return w.astype(BF16), we.astype(BF16), wf.astype(BF16), at, dd


def _ssm_state_kernel(u_ref, we_ref, sre_ref, sim_ref):
    gp = u_ref.shape[0]
    for d in range(2):
        for part, dst in ((0, sre_ref), (1, sim_ref)):
            acc = jnp.dot(u_ref[0], we_ref[d, part, 0], preferred_element_type=F32)
            for q in range(1, gp):
                acc = acc + jnp.dot(u_ref[q], we_ref[d, part, q], preferred_element_type=F32)
            dst[d] = acc


def _ssm_scan_kernel(sre_ref, sim_ref, at_ref, hre_ref, him_ref, *, bsz, nch, nctx):
    d = pl.program_id(0)
    atr, ati = at_ref[0:1, :], at_ref[1:2, :]
    zero = jnp.zeros(atr.shape, F32)

    def body(k, carry):
        c_rev = jnp.where(k < nctx, nctx - 1 - k, nch - 1 - (k - nctx))
        c = jnp.where(d == 0, k, c_rev)
        new = []
        for b in range(bsz):
            hr, hi = carry[2 * b], carry[2 * b + 1]
            row = pl.ds(b * nch + c, 1)
            hre_ref[row, :] = hr
            him_ref[row, :] = hi
            new.append(atr * hr - ati * hi + sre_ref[row, :])
            new.append(atr * hi + ati * hr + sim_ref[row, :])
        return tuple(new)

    lax.fori_loop(0, nch, body, tuple([zero] * (2 * bsz)))


def _ssm_out_kernel(u_ref, w_ref, wf_ref, hre_ref, him_ref, dd_ref, y_ref):
    gp = u_ref.shape[0]
    for q in range(gp):
        u = u_ref[q]
        y = u.astype(F32) * dd_ref[q]
        for d in range(2):
            y = y + jnp.dot(u, w_ref[d, q], preferred_element_type=F32)
            y = y + jnp.dot(hre_ref[d].astype(BF16), wf_ref[d, 0, q], preferred_element_type=F32)
            y = y + jnp.dot(him_ref[d].astype(BF16), wf_ref[d, 1, q], preferred_element_type=F32)
        y_ref[q] = y.astype(y_ref.dtype)


def ssm_branch(su, mats, *, ctx_len):
    w, we, wf, at, dd = mats
    bsz, t, width = su.shape
    tc = SSM_CHUNK
    n_g = w.shape[1]
    n_i = width // n_g
    gpp = we.shape[-1]
    n_p = at.shape[-1] // n_g
    gp = gpp // n_p
    nch = t // tc
    mc = bsz * nch
    ug = su.reshape(bsz, nch, tc, n_g, n_i).transpose(3, 0, 1, 2, 4).reshape(n_g, mc, tc * n_i)

    s_spec = pl.BlockSpec((2, mc, gpp), lambda g: (0, 0, g))
    s_shape = jax.ShapeDtypeStruct((2, mc, n_g * n_p), F32)
    sre, sim = pl.pallas_call(
        _ssm_state_kernel,
        grid=(n_g // gp,),
        in_specs=[pl.BlockSpec((gp, mc, tc * n_i), lambda g: (g, 0, 0)),
                  pl.BlockSpec((2, 2, gp, tc * n_i, gpp), lambda g: (0, 0, g, 0, 0))],
        out_specs=[s_spec, s_spec],
        out_shape=[s_shape, s_shape],
        compiler_params=_params(("parallel",)),
        name="ssm_chunk_state",
    )(ug, we)

    lanes = _divisor(n_g * n_p, 4 * V7X_LANES, V7X_LANES)
    st_spec = pl.BlockSpec((None, mc, lanes), lambda di, j: (di, 0, j))
    hre, him = pl.pallas_call(
        functools.partial(_ssm_scan_kernel, bsz=bsz, nch=nch, nctx=ctx_len // tc),
        grid=(2, n_g * n_p // lanes),
        in_specs=[st_spec, st_spec, pl.BlockSpec((None, 2, lanes), lambda di, j: (di, 0, j))],
        out_specs=[st_spec, st_spec],
        out_shape=[s_shape, s_shape],
        compiler_params=_params(("parallel", "parallel")),
        name="ssm_chunk_scan",
    )(sre, sim, at)

    yg = pl.pallas_call(
        _ssm_out_kernel,
        grid=(n_g // gp,),
        in_specs=[pl.BlockSpec((gp, mc, tc * n_i), lambda g: (g, 0, 0)),
                  pl.BlockSpec((2, gp, tc * n_i, tc * n_i), lambda g: (0, g, 0, 0)),
                  pl.BlockSpec((2, 2, gp, gpp, tc * n_i), lambda g: (0, 0, g, 0, 0)),
                  s_spec, s_spec,
                  pl.BlockSpec((gp, 1, tc * n_i), lambda g: (g, 0, 0))],
        out_specs=pl.BlockSpec((gp, mc, tc * n_i), lambda g: (g, 0, 0)),
        out_shape=jax.ShapeDtypeStruct((n_g, mc, tc * n_i), BF16),
        compiler_params=_params(("parallel",)),
        name="ssm_chunk_out",
    )(ug, w, wf, hre, him, dd)
    return yg.reshape(n_g, bsz, nch, tc, n_i).transpose(1, 2, 3, 0, 4).reshape(bsz * t, width)


def _glu_kernel(y_ref, w_ref, b_ref, o_ref):
    y = y_ref[...].astype(F32)
    g = 0.5 * y * (1.0 + jnp.tanh(math.sqrt(2.0 / math.pi) * (y + 0.044715 * (y * y * y))))
    r = jnp.dot(g.astype(BF16), w_ref[...], preferred_element_type=F32) + b_ref[...]
    o_ref[...] = (g * jax.nn.sigmoid(r)).astype(o_ref.dtype)


def s5_glu(y, w, b):
    m, width = y.shape
    tm = _divisor(m, 768, 16)
    return pl.pallas_call(
        _glu_kernel,
        grid=(m // tm,),
        in_specs=[pl.BlockSpec((tm, width), lambda i: (i, 0)),
                  pl.BlockSpec((width, width), lambda i: (0, 0)),
                  pl.BlockSpec((1, width), lambda i: (0, 0))],
        out_specs=pl.BlockSpec((tm, width), lambda i: (i, 0)),
        out_shape=jax.ShapeDtypeStruct((m, width), BF16),
        compiler_params=_params(("parallel",)),
        name="s5_glu",
    )(y, w, b.reshape(1, width))


def _merge_kernel(att_ref, conv_ref, ssm_ref, g0_ref, g1_ref, g2_ref, pa_ref, pc_ref, ps_ref, o_ref):
    def branch(x_ref, p_ref, g_ref):
        y = jnp.dot(x_ref[...], p_ref[...], preferred_element_type=F32)
        return jax.nn.sigmoid(g_ref[...].astype(F32)) * y

    m = branch(att_ref, pa_ref, g0_ref) + branch(conv_ref, pc_ref, g1_ref) + branch(ssm_ref, ps_ref, g2_ref)
    o_ref[...] = m.astype(o_ref.dtype)


def branch_merge(att, conv, ssm, z, p_att, p_conv, p_ssm, *, g_off):
    m, d = att.shape[0], p_att.shape[1]
    tm = _divisor(m, 768, 16)
    tn = _divisor(math.gcd(d, g_off), 1024, V7X_LANES)
    gblk = g_off // tn
    nj = d // tn
    row = lambda a: pl.BlockSpec((tm, a.shape[1]), lambda i, j: (i, 0))
    gate = lambda k: pl.BlockSpec((tm, tn), lambda i, j: (i, gblk + k * nj + j))
    col = lambda p: pl.BlockSpec((p.shape[0], tn), lambda i, j: (0, j))
    return pl.pallas_call(
        _merge_kernel,
        grid=(m // tm, nj),
        in_specs=[row(att), row(conv), row(ssm), gate(0), gate(1), gate(2), col(p_att), col(p_conv), col(p_ssm)],
        out_specs=pl.BlockSpec((tm, tn), lambda i, j: (i, j)),
        out_shape=jax.ShapeDtypeStruct((m, d), BF16),
        compiler_params=_params(("parallel", "parallel")),
        name="branch_merge",
    )(att, conv, ssm, z, z, z, p_att, p_conv, p_ssm)


def _first_max(x, axis, iota):
    mx = jnp.max(x, axis=axis, keepdims=True)
    n = x.shape[axis]
    idx = jnp.min(jnp.where(x == mx, iota, n), axis=axis, keepdims=True)
    return mx, idx, iota == idx


def _router_kernel(lg_ref, b_ref, idx_ref, wgt_ref):
    n_e, tm = lg_ref.shape
    per = n_e // N_EXPERT_GROUPS
    scores = jax.nn.sigmoid(lg_ref[...])
    sel = scores + b_ref[...]
    neg = -jnp.inf
    grp = sel.reshape(N_EXPERT_GROUPS, per, tm)
    iota_e = lax.broadcasted_iota(jnp.int32, grp.shape, 1)
    m1, _, first = _first_max(grp, 1, iota_e)
    m2 = jnp.max(jnp.where(first, neg, grp), axis=1, keepdims=True)
    gscore = m1 + m2
    iota_g = lax.broadcasted_iota(jnp.int32, gscore.shape, 0)
    gmask = jnp.zeros(gscore.shape, F32)
    for _ in range(TOPK_GROUPS):
        _, _, hit = _first_max(gscore, 0, iota_g)
        gmask = jnp.where(hit, 1.0, gmask)
        gscore = jnp.where(hit, neg, gscore)
    cand = jnp.where(jnp.broadcast_to(gmask, grp.shape) > 0.0, grp, neg).reshape(n_e, tm)
    iota_x = lax.broadcasted_iota(jnp.int32, cand.shape, 0)
    chosen = jnp.zeros(cand.shape, F32)
    for r in range(TOP_K):
        _, idx, hit = _first_max(cand, 0, iota_x)
        chosen = jnp.where(hit, 1.0, chosen)
        cand = jnp.where(hit, neg, cand)
        idx_ref[r:r + 1, :] = idx
        wgt_ref[r:r + 1, :] = jnp.sum(jnp.where(hit, scores, 0.0), axis=0, keepdims=True)
    norm = ROUTED_SCALE / jnp.sum(chosen * scores, axis=0, keepdims=True)
    wgt_ref[...] = wgt_ref[...] * norm


def route(logits_t, router_b):
    n_e, m = logits_t.shape
    tm = _divisor(m, 512, V7X_LANES)
    spec = lambda r: pl.BlockSpec((r, tm), lambda i: (0, i))
    return pl.pallas_call(
        _router_kernel,
        grid=(m // tm,),
        in_specs=[spec(n_e), pl.BlockSpec((n_e, 1), lambda i: (0, 0))],
        out_specs=[spec(TOP_K), spec(TOP_K)],
        out_shape=[jax.ShapeDtypeStruct((TOP_K, m), jnp.int32), jax.ShapeDtypeStruct((TOP_K, m), F32)],
        compiler_params=_params(("parallel",)),
        name="moe_route",
    )(logits_t, router_b.reshape(n_e, 1))


MOE_TILE = 256


def _rank_kernel(eidx_ref, rank_ref, cnt_ref):
    @pl.when(pl.program_id(0) == 0)
    def _():
        cnt_ref[...] = jnp.zeros(cnt_ref.shape, F32)

    n_e = cnt_ref.shape[0]
    tm = eidx_ref.shape[1]
    earlier = (lax.broadcasted_iota(jnp.int32, (tm, tm), 0) < lax.broadcasted_iota(jnp.int32, (tm, tm), 1))
    upper = jnp.where(earlier, 1.0, 0.0).astype(BF16)
    iota_e = lax.broadcasted_iota(jnp.int32, (n_e, tm), 0)
    base = cnt_ref[...]
    for r in range(TOP_K):
        onehot = jnp.where(iota_e == eidx_ref[r:r + 1, :], 1.0, 0.0)
        before = jnp.dot(onehot.astype(BF16), upper, preferred_element_type=F32)
        rank_ref[r:r + 1, :] = jnp.sum(onehot * (base + before), axis=0, keepdims=True).astype(jnp.int32)
        base = base + jnp.sum(onehot, axis=1, keepdims=True)
    cnt_ref[...] = base


def moe_rank(eidx, n_e):
    _, m = eidx.shape
    tm = _divisor(m, 256, V7X_LANES)
    return pl.pallas_call(
        _rank_kernel,
        grid=(m // tm,),
        in_specs=[pl.BlockSpec((TOP_K, tm), lambda i: (0, i))],
        out_specs=[pl.BlockSpec((TOP_K, tm), lambda i: (0, i)), pl.BlockSpec((n_e, 1), lambda i: (0, 0))],
        out_shape=[jax.ShapeDtypeStruct((TOP_K, m), jnp.int32), jax.ShapeDtypeStruct((n_e, 1), F32)],
        compiler_params=_params(("arbitrary",)),
        name="moe_rank",
    )(eidx)


def _foreach(lo, hi, fn):
    def body(j, carry):
        fn(j)
        return carry

    lax.fori_loop(lo, hi, body, 0)


def _dispatch_kernel(pad_ref, pos_ref, x_ref, xs_ref, zero_ref, sem, zsem, *, n_e):
    tm = x_ref.shape[0]
    tr = zero_ref.shape[0]
    n_tiles = xs_ref.shape[0] // tr

    def zero_row_copy(row):
        return pltpu.make_async_copy(zero_ref.at[pl.ds(0, 1)], xs_ref.at[pl.ds(row, 1)], zsem)

    def zero_tile_copy(tile):
        return pltpu.make_async_copy(zero_ref, xs_ref.at[pl.ds(tile * tr, tr)], zsem)

    @pl.when(pl.program_id(0) == 0)
    def _():
        zero_ref[...] = jnp.zeros(zero_ref.shape, zero_ref.dtype)

        first_unused = pad_ref[2 * n_e]
        _foreach(0, n_e, lambda e: _foreach(0, pad_ref[n_e + e], lambda j: zero_row_copy(pad_ref[e] + j).start()))
        _foreach(first_unused, n_tiles, lambda t: zero_tile_copy(t).start())
        _foreach(0, n_e, lambda e: _foreach(0, pad_ref[n_e + e], lambda j: zero_row_copy(0).wait()))
        _foreach(first_unused, n_tiles, lambda t: zero_tile_copy(0).wait())

    def scatter_token(n):
        for r in range(TOP_K):
            pltpu.make_async_copy(x_ref.at[pl.ds(n, 1)], xs_ref.at[pl.ds(pos_ref[r, n], 1)], sem).start()

    _foreach(0, tm, scatter_token)
    for r in range(TOP_K):
        pltpu.make_async_copy(x_ref, xs_ref.at[pl.ds(0, tm)], sem).wait()


def moe_dispatch(hp, pos, pad_info, n_rows, n_e):
    m, w = hp.shape
    tm = _divisor(m, 256, V7X_LANES)
    grid_spec = pltpu.PrefetchScalarGridSpec(
        num_scalar_prefetch=1,
        grid=(m // tm,),
        in_specs=[pl.BlockSpec((TOP_K, tm), lambda i, pad: (0, i), memory_space=pltpu.SMEM),
                  pl.BlockSpec((tm, w), lambda i, pad: (i, 0))],
        out_specs=pl.BlockSpec(memory_space=pl.ANY),
        scratch_shapes=[pltpu.VMEM((MOE_TILE, w), jnp.uint32), pltpu.SemaphoreType.DMA(()),
                        pltpu.SemaphoreType.DMA(())],
    )
    return pl.pallas_call(
        functools.partial(_dispatch_kernel, n_e=n_e),
        grid_spec=grid_spec,
        out_shape=jax.ShapeDtypeStruct((n_rows, w), jnp.uint32),
        compiler_params=_params(("arbitrary",)),
        name="moe_dispatch",
    )(pad_info, pos, hp)


def _moe_ffn_kernel(te_ref, nv_ref, xs_ref, wg_ref, wu_ref, wd_ref, ys_ref, wgb_ref, wub_ref, wdb_ref):
    i = pl.program_id(0)
    nv = nv_ref[i]
    new_expert = jnp.logical_or(i == 0, te_ref[i] != te_ref[jnp.maximum(i - 1, 0)])

    @pl.when(jnp.logical_and(new_expert, nv > 0))
    def _():
        wgb_ref[...] = wg_ref[...].astype(BF16)
        wub_ref[...] = wu_ref[...].astype(BF16)
        wdb_ref[...] = wd_ref[...].astype(BF16)

    @pl.when(nv > 0)
    def _():
        rows = lax.broadcasted_iota(jnp.int32, (xs_ref.shape[0], 1), 0)
        packed = jnp.where(rows < nv, xs_ref[...], jnp.uint32(0))
        lo, hi = _unpack_halves(packed)
        x = jnp.concatenate([lo.astype(BF16), hi.astype(BF16)], axis=1)
        g = jnp.dot(x, wgb_ref[...], preferred_element_type=F32)
        u = jnp.dot(x, wub_ref[...], preferred_element_type=F32)
        hid = (g * jax.nn.sigmoid(g) * u).astype(BF16)
        y = jnp.dot(hid, wdb_ref[...], preferred_element_type=F32)
        half = y.shape[1] // 2
        ys_ref[...] = _pack_halves(y[:, :half], y[:, half:])

    @pl.when(nv == 0)
    def _():
        ys_ref[...] = jnp.zeros(ys_ref.shape, jnp.uint32)


def moe_ffn(xs, tile_expert, tile_rows, w_gate, w_up, w_down):
    n_rows, w = xs.shape
    _, d, ff = w_gate.shape
    tr = MOE_TILE
    grid_spec = pltpu.PrefetchScalarGridSpec(
        num_scalar_prefetch=2,
        grid=(n_rows // tr,),
        in_specs=[pl.BlockSpec((tr, w), lambda i, te, nv: (i, 0)),
                  pl.BlockSpec((None, d, ff), lambda i, te, nv: (te[i], 0, 0)),
                  pl.BlockSpec((None, d, ff), lambda i, te, nv: (te[i], 0, 0)),
                  pl.BlockSpec((None, ff, d), lambda i, te, nv: (te[i], 0, 0))],
        out_specs=pl.BlockSpec((tr, w), lambda i, te, nv: (i, 0)),
        scratch_shapes=[pltpu.VMEM((d, ff), BF16), pltpu.VMEM((d, ff), BF16), pltpu.VMEM((ff, d), BF16)],
    )
    return pl.pallas_call(
        _moe_ffn_kernel,
        grid_spec=grid_spec,
        out_shape=jax.ShapeDtypeStruct((n_rows, w), jnp.uint32),
        compiler_params=_params(("arbitrary",)),
        name="moe_ffn",
    )(tile_expert, tile_rows, xs, w_gate, w_up, w_down)


def _combine_kernel(pos_ref, w_ref, sh_ref, ys_ref, o_ref, buf_ref, sem, *, cw):
    tm, half = buf_ref.shape[1], buf_ref.shape[2]

    def gather_token(n):
        for r in range(TOP_K):
            pltpu.make_async_copy(ys_ref.at[pl.ds(pos_ref[r, n], 1)], buf_ref.at[r, pl.ds(n, 1)], sem).start()

    _foreach(0, tm, gather_token)
    for r in range(TOP_K):
        pltpu.make_async_copy(ys_ref.at[pl.ds(0, tm)], buf_ref.at[r], sem).wait()
    for c0 in range(0, half, cw):
        acc_lo = sh_ref[:, c0:c0 + cw].astype(F32)
        acc_hi = sh_ref[:, half + c0:half + c0 + cw].astype(F32)
        for r in range(TOP_K):
            lo, hi = _unpack_halves(buf_ref[r, :, c0:c0 + cw])
            wr = w_ref[:, r:r + 1]
            acc_lo = acc_lo + wr * lo
            acc_hi = acc_hi + wr * hi
        o_ref[:, c0:c0 + cw] = acc_lo.astype(o_ref.dtype)
        o_ref[:, half + c0:half + c0 + cw] = acc_hi.astype(o_ref.dtype)


def moe_combine(ys, pos, wgt_t, shared):
    m, d = shared.shape
    w = ys.shape[1]
    tm = _divisor(m, 128, V7X_LANES)
    return pl.pallas_call(
        functools.partial(_combine_kernel, cw=_divisor(w, 256, V7X_LANES)),
        grid=(m // tm,),
        in_specs=[pl.BlockSpec((TOP_K, tm), lambda i: (0, i), memory_space=pltpu.SMEM),
                  pl.BlockSpec((tm, TOP_K), lambda i: (i, 0)),
                  pl.BlockSpec((tm, d), lambda i: (i, 0)),
                  pl.BlockSpec(memory_space=pl.ANY)],
        out_specs=pl.BlockSpec((tm, d), lambda i: (i, 0)),
        out_shape=jax.ShapeDtypeStruct((m, d), BF16),
        scratch_shapes=[pltpu.VMEM((TOP_K, tm, w), jnp.uint32), pltpu.SemaphoreType.DMA(())],
        compiler_params=_params(("arbitrary",)),
        name="moe_combine",
    )(pos, wgt_t, shared, ys)


def _ffn_up_kernel(x_ref, wg_ref, wu_ref, o_ref):
    x = x_ref[...]
    g = jnp.dot(x, wg_ref[...], preferred_element_type=F32)
    u = jnp.dot(x, wu_ref[...], preferred_element_type=F32)
    o_ref[...] = (g * jax.nn.sigmoid(g) * u).astype(o_ref.dtype)


def ffn_up(h, w_gate, w_up):
    m, d = h.shape
    ff = w_gate.shape[1]
    tm = _divisor(m, 768, 16)
    wspec = pl.BlockSpec((d, ff), lambda i: (0, 0))
    return pl.pallas_call(
        _ffn_up_kernel,
        grid=(m // tm,),
        in_specs=[pl.BlockSpec((tm, d), lambda i: (i, 0)), wspec, wspec],
        out_specs=pl.BlockSpec((tm, ff), lambda i: (i, 0)),
        out_shape=jax.ShapeDtypeStruct((m, ff), BF16),
        compiler_params=_params(("parallel",)),
        name="ffn_up",
    )(h, w_gate, w_up)


def moe_sparse(h2, h2p, eidx, wgt, w_gate, w_up, w_down, ws_gate, ws_up, ws_down):
    m, d = h2.shape
    n_e = w_gate.shape[0]
    tr = MOE_TILE
    rank, cnt = moe_rank(eidx, n_e)
    counts = cnt[:, 0].astype(jnp.int32)
    tiles_per = (counts + (tr - 1)) // tr
    tile_end = jnp.cumsum(tiles_per)
    tile_start = tile_end - tiles_per
    pos = rank + jnp.take(tile_start * tr, eidx)
    n_tiles = (TOP_K * m) // tr + n_e
    t_idx = jnp.arange(n_tiles, dtype=jnp.int32)
    tile_expert = jnp.minimum(jnp.searchsorted(tile_end, t_idx, side='right'), n_e - 1).astype(jnp.int32)
    tile_rows = jnp.clip(counts[tile_expert] - (t_idx - tile_start[tile_expert]) * tr, 0, tr).astype(jnp.int32)
    pad_info = jnp.concatenate([tile_start * tr + counts, tiles_per * tr - counts, tile_end[-1:]]).astype(jnp.int32)
    xs = moe_dispatch(h2p, pos, pad_info, n_tiles * tr, n_e)
    ys = moe_ffn(xs, tile_expert, tile_rows, w_gate, w_up, w_down)
    shared = matmul(ffn_up(h2, ws_gate, ws_up), ws_down)
    return moe_combine(ys, pos, wgt.T, shared)


def _layer_weights(i, w_in, p_att, p_conv, p_ssm, w_o, glu_w, router_w, sh_w_gate, sh_w_up, sh_w_down):
    rw_t = router_w[i].T
    rw_hi = rw_t.astype(BF16)
    rw_lo = (rw_t - rw_hi.astype(F32)).astype(BF16)
    return dict(w_in=w_in[i].astype(BF16), p_att=p_att[i].astype(BF16), p_conv=p_conv[i].astype(BF16),
                p_ssm=p_ssm[i].astype(BF16), w_o=w_o[i].astype(BF16), glu_w=glu_w[i].astype(BF16),
                router=(rw_hi, rw_lo), ws_gate=sh_w_gate[i].astype(BF16), ws_up=sh_w_up[i].astype(BF16),
                ws_down=sh_w_down[i].astype(BF16))


def kernel(x, c, ctx, c_ctx, ada_w, ada_b, norm_g, w_in, lam_vec, att_g, conv_w,
           ssm_a_re, ssm_a_im, ssm_log_dt, ssm_b_re, ssm_b_im, ssm_c_re, ssm_c_im, ssm_d,
           glu_w, glu_b, p_att, p_conv, p_ssm, w_o, router_w, router_b,
           exp_w_gate, exp_w_up, exp_w_down, sh_w_gate, sh_w_up, sh_w_down):
    bsz, n_lat, d = x.shape
    n_ctx = ctx.shape[1]
    t = n_ctx + n_lat
    m = bsz * t
    depth = ada_w.shape[0]
    att_w = p_att.shape[1]
    heads = att_w // ATT_VDIM
    conv_wd = conv_w.shape[-1]
    ssm_wd = ssm_d.shape[-1]
    n_e = router_w.shape[-1]
    n_p = ssm_a_re.shape[-1]
    assert bsz + 1 <= 8 and n_ctx % SSM_CHUNK == 0 and n_lat % SSM_CHUNK == 0 and n_lat % GRID_W == 0
    assert V7X_LANES % n_p == 0 and n_e % N_EXPERT_GROUPS == 0
    k_off = att_w
    v_off = 2 * att_w
    cb_off = 3 * att_w
    su_off = cb_off + 3 * conv_wd
    g_off = su_off + ssm_wd

    stream = jnp.concatenate([ctx, x], axis=1)
    cond_cols = jnp.zeros((d, 8), F32).at[:, :bsz].set(c.T).at[:, bsz].set(c_ctx)
    tables = rope_tables(n_ctx, n_lat)

    pending = None
    for i in range(depth):
        lw = _layer_weights(i, w_in, p_att, p_conv, p_ssm, w_o, glu_w, router_w, sh_w_gate, sh_w_up, sh_w_down)
        lam_init = 0.8 - 0.6 * math.exp(-0.3 * i)
        mod8 = modulation(cond_cols, ada_w[i], ada_b[i], bsz + 1).reshape(8, N_MOD, d)
        mod = jnp.stack([jnp.broadcast_to(mod8[bsz], (bsz, N_MOD, d)), mod8[:bsz]], axis=1)
        gains = norm_g[i]

        if pending is None:
            (h,) = resid_adaln(stream, None, mod, gains, ctx_len=n_ctx, pre_idx=0, shift_idx=0, scale_idx=1)
        else:
            branch, p_mod, p_gains = pending
            stream, h = resid_adaln(stream, branch, jnp.concatenate([mod, p_mod], axis=2),
                                    jnp.concatenate([gains, p_gains], axis=0), ctx_len=n_ctx,
                                    post_idx=4 + 3, gate_idx=N_MOD + 5, pre_idx=0, shift_idx=0, scale_idx=1)
        z = matmul(h.reshape(m, d), lw['w_in'])
        z3 = z.reshape(bsz, t, -1)

        qr, kr = rope_qk(z3, tables, att_w, n_ctx)
        att = jnp.zeros((bsz, t, att_w), BF16)
        att = diff_attention(qr, kr, z3, lam_vec[i], att_g[i], v_col_blk=v_off // ATT_VDIM, q_row0=0, n_q=n_ctx,
                             n_kv=n_ctx, lam_init=lam_init, out=att, heads=heads)
        att = diff_attention(qr, kr, z3, lam_vec[i], att_g[i], v_col_blk=v_off // ATT_VDIM, q_row0=n_ctx, n_q=n_lat,
                             n_kv=t, lam_init=lam_init, out=att, heads=heads)

        conv = short_conv(z3, conv_w[i], cb_off=cb_off, ctx_len=n_ctx)

        mats = ssm_matrices(ssm_a_re[i], ssm_a_im[i], ssm_log_dt[i], ssm_b_re[i], ssm_b_im[i],
                            ssm_c_re[i], ssm_c_im[i], ssm_d[i], V7X_LANES // n_p)
        y = ssm_branch(z3[:, :, su_off:su_off + ssm_wd], mats, ctx_len=n_ctx)
        ssm = s5_glu(y, lw['glu_w'], glu_b[i])

        merged = branch_merge(att.reshape(m, att_w), conv.reshape(m, conv_wd), ssm, z,
                              lw['p_att'], lw['p_conv'], lw['p_ssm'], g_off=g_off)
        o = matmul(merged, lw['w_o']).reshape(bsz, t, d)

        stream, h2, h2p, logits_t = resid_adaln(stream, o, mod, gains, ctx_len=n_ctx, post_idx=1, gate_idx=2,
                                                pre_idx=2, shift_idx=3, scale_idx=4, router=lw['router'])
        eidx, wgt = route(logits_t, router_b[i])
        f = moe_sparse(h2.reshape(m, d), h2p.reshape(m, d // 2), eidx, wgt, exp_w_gate[i], exp_w_up[i],
                       exp_w_down[i], lw['ws_gate'], lw['ws_up'], lw['ws_down']).reshape(bsz, t, d)
        pending = (f, mod, gains)

    branch, p_mod, p_gains = pending
    (out,) = resid_adaln(stream, branch, p_mod, p_gains, ctx_len=n_ctx, post_idx=3, gate_idx=5,
                         latent_only_out=True)
    return out
```

```python
import functools
import math

import jax
import jax.numpy as jnp
from jax import lax
from jax.experimental import pallas as pl
from jax.experimental.pallas import tpu as pltpu

F32 = jnp.float32
BF16 = jnp.bfloat16

GRID_W = 64
ROPE_BASE = 10000.0
ATT_DIM = 128
ATT_VDIM = 2 * ATT_DIM
N_BRANCH = 3
TOP_K = 8
N_EXPERT_GROUPS = 8
TOPK_GROUPS = 4
ROUTED_SCALE = 2.5
N_MOD = 6
EPS = 1e-6
SSM_CHUNK = 16
ATT_MAX_UNROLL = 16

V7X_LANES = 128
V7X_VMEM_LIMIT_BYTES = 56 * 1024 * 1024
ROW_BLOCK = 256


def _divisor(n, target, mult):
    best = None
    for d in range(mult, min(n, target) + 1, mult):
        if n % d == 0:
            best = d
    return best if best is not None else n


def _params(sem):
    return pltpu.CompilerParams(dimension_semantics=sem, vmem_limit_bytes=V7X_VMEM_LIMIT_BYTES)


def _mod_kernel(s_ref, w_ref, b_ref, o_ref, *, rows, kc):
    s = s_ref[...]
    s = s * jax.nn.sigmoid(s)
    o_ref[...] = jnp.zeros(o_ref.shape, F32)
    d = w_ref.shape[0]
    for r in range(rows):
        acc = b_ref[...]
        for k0 in range(0, d, kc):
            acc = acc + jnp.sum(w_ref[k0:k0 + kc, :] * s[k0:k0 + kc, r:r + 1], axis=0, keepdims=True)
        o_ref[r:r + 1, :] = acc


def modulation(cond_cols, w, b, rows, layer):
    n_l, d, n = w.shape
    tn = _divisor(n, 512, V7X_LANES)
    kc = _divisor(d, 512, 8)
    return pl.pallas_call(
        functools.partial(_mod_kernel, rows=rows, kc=kc),
        grid=(n // tn,),
        in_specs=[pl.BlockSpec((d, 8), lambda j: (0, 0)),
                  pl.BlockSpec((None, d, tn), lambda j: (layer, 0, j)),
                  pl.BlockSpec((None, 1, tn), lambda j: (layer, 0, j))],
        out_specs=pl.BlockSpec((8, tn), lambda j: (0, j)),
        out_shape=jax.ShapeDtypeStruct((8, n), F32),
        compiler_params=_params(("arbitrary",)),
        name="modulation",
    )(cond_cols, w, b.reshape(n_l, 1, n))


def _rms(x, g):
    return x * lax.rsqrt(jnp.mean(x * x, axis=-1, keepdims=True) + EPS) * g


def _pack_halves(a, b):
    lo = lax.bitcast_convert_type(a.astype(BF16).astype(F32), jnp.uint32) >> 16
    hi = lax.bitcast_convert_type(b.astype(BF16).astype(F32), jnp.uint32) & jnp.uint32(0xFFFF0000)
    return lo | hi


def _unpack_halves(p):
    lo = lax.bitcast_convert_type(p << 16, F32)
    hi = lax.bitcast_convert_type(p & jnp.uint32(0xFFFF0000), F32)
    return lo, hi


def _resid_adaln_kernel(*refs, has_branch, has_h, has_router, post_idx, gate_idx, pre_idx, shift_idx, scale_idx):
    it = iter(refs)
    x_ref = next(it)
    o_ref = next(it) if has_branch else None
    mod_ref = next(it)
    g_ref = next(it)
    rwh_ref = next(it) if has_router else None
    rwl_ref = next(it) if has_router else None
    xout_ref = next(it) if has_branch else None
    h_ref = next(it) if has_h else None
    hp_ref = next(it) if has_router else None
    lg_ref = next(it) if has_router else None

    x = x_ref[...]
    if has_branch:
        o = o_ref[...].astype(F32)
        x = x + mod_ref[gate_idx:gate_idx + 1, :] * _rms(o, g_ref[post_idx:post_idx + 1, :])
        xout_ref[...] = x
    if has_h:
        h = _rms(x, g_ref[pre_idx:pre_idx + 1, :])
        h = h * (1.0 + mod_ref[scale_idx:scale_idx + 1, :]) + mod_ref[shift_idx:shift_idx + 1, :]
        h_ref[...] = h.astype(BF16)
        if has_router:
            h_hi = h.astype(BF16)
            h_lo = (h - h_hi.astype(F32)).astype(BF16)
            nt = (((1,), (1,)), ((), ()))
            lg = lax.dot_general(rwh_ref[...], h_hi, nt, preferred_element_type=F32)
            lg = lg + lax.dot_general(rwh_ref[...], h_lo, nt, preferred_element_type=F32)
            lg = lg + lax.dot_general(rwl_ref[...], h_hi, nt, preferred_element_type=F32)
            lg_ref[...] = lg
            half = h.shape[1] // 2
            hp_ref[...] = _pack_halves(h[:, :half], h[:, half:])


def resid_adaln(x, branch, mod, gains, *, ctx_len, post_idx=0, gate_idx=0, pre_idx=None,
                shift_idx=0, scale_idx=0, router=None, latent_only_out=False):
    bsz, t, d = x.shape
    rb = _divisor(math.gcd(ctx_len, t - ctx_len), ROW_BLOCK, 8)
    nctx = ctx_len // rb
    has_branch = branch is not None
    has_h = pre_idx is not None
    has_router = router is not None
    off = nctx if latent_only_out else 0
    nblk = t // rb - off

    def row_map(b, i):
        return (b, i + off, 0)

    in_specs = [pl.BlockSpec((None, rb, d), row_map)]
    args = [x]
    if has_branch:
        in_specs.append(pl.BlockSpec((None, rb, d), row_map))
        args.append(branch)
    in_specs.append(pl.BlockSpec((None, None, mod.shape[2], d),
                                 lambda b, i: (b, jnp.where(i + off >= nctx, 1, 0), 0, 0)))
    args.append(mod)
    in_specs.append(pl.BlockSpec(gains.shape, lambda b, i: (0, 0)))
    args.append(gains)
    if has_router:
        rw_hi, rw_lo = router
        in_specs += [pl.BlockSpec(rw_hi.shape, lambda b, i: (0, 0)), pl.BlockSpec(rw_lo.shape, lambda b, i: (0, 0))]
        args += [rw_hi, rw_lo]
    out_specs, out_shape = [], []
    if has_branch:
        out_specs.append(pl.BlockSpec((None, rb, d), lambda b, i: (b, i, 0)))
        out_shape.append(jax.ShapeDtypeStruct((bsz, nblk * rb, d), F32))
    if has_h:
        out_specs.append(pl.BlockSpec((None, rb, d), lambda b, i: (b, i, 0)))
        out_shape.append(jax.ShapeDtypeStruct((bsz, t, d), BF16))
    if has_router:
        n_e = router[0].shape[0]
        out_specs.append(pl.BlockSpec((None, rb, d // 2), lambda b, i: (b, i, 0)))
        out_shape.append(jax.ShapeDtypeStruct((bsz, t, d // 2), jnp.uint32))
        out_specs.append(pl.BlockSpec((n_e, rb), lambda b, i: (0, b * nblk + i)))
        out_shape.append(jax.ShapeDtypeStruct((n_e, bsz * t), F32))
    return pl.pallas_call(
        functools.partial(_resid_adaln_kernel, has_branch=has_branch, has_h=has_h, has_router=has_router,
                          post_idx=post_idx, gate_idx=gate_idx, pre_idx=pre_idx, shift_idx=shift_idx,
                          scale_idx=scale_idx),
        grid=(bsz, nblk),
        in_specs=in_specs,
        out_specs=out_specs,
        out_shape=out_shape,
        compiler_params=_params(("parallel", "parallel")),
        name="resid_adaln",
    )(*args)


def _mm_kernel(a_ref, b_ref, o_ref):
    o_ref[...] = jnp.dot(a_ref[...], b_ref[...], preferred_element_type=F32).astype(o_ref.dtype)


def matmul(a, b, layer, out_dtype=BF16, tm_target=768, tn_target=1024):
    m, k = a.shape
    _, _, n = b.shape
    tm = _divisor(m, tm_target, 16)
    tn = _divisor(n, tn_target, V7X_LANES)
    return pl.pallas_call(
        _mm_kernel,
        grid=(m // tm, n // tn),
        in_specs=[pl.BlockSpec((tm, k), lambda i, j: (i, 0)),
                  pl.BlockSpec((None, k, tn), lambda i, j: (layer, 0, j))],
        out_specs=pl.BlockSpec((tm, tn), lambda i, j: (i, j)),
        out_shape=jax.ShapeDtypeStruct((m, n), out_dtype),
        compiler_params=_params(("parallel", "parallel")),
        name="matmul",
    )(a, b)


def rope_tables(ctx_len, n_lat):
    nf = ATT_DIM // 4
    rows = n_lat // GRID_W
    row = jnp.repeat(jnp.arange(rows), GRID_W).astype(F32)
    col = jnp.tile(jnp.arange(GRID_W), rows).astype(F32)
    inv = ROPE_BASE ** (-jnp.arange(nf, dtype=F32) / nf)
    ang_r = row[:, None] * inv
    ang_c = col[:, None] * inv
    zero = jnp.zeros_like(ang_r)
    cos = jnp.concatenate([jnp.cos(ang_r), jnp.cos(ang_r), jnp.cos(ang_c), jnp.cos(ang_c)], axis=-1)
    sa = jnp.concatenate([-jnp.sin(ang_r), zero, -jnp.sin(ang_c), zero], axis=-1)
    sb = jnp.concatenate([zero, jnp.sin(ang_r), zero, jnp.sin(ang_c)], axis=-1)
    pad = lambda tbl, v: jnp.concatenate([jnp.full((ctx_len, ATT_DIM), v, F32), tbl], axis=0)
    return pad(cos, 1.0), pad(sa, 0.0), pad(sb, 0.0)


def _rope_kernel(q_ref, k_ref, cos_ref, sa_ref, sb_ref, qo_ref, ko_ref, *, q_scale):
    cos, sa, sb = cos_ref[...], sa_ref[...], sb_ref[...]
    quarter = ATT_DIM // 4
    for src, dst, scale in ((q_ref, qo_ref, q_scale), (k_ref, ko_ref, 1.0)):
        for g0 in range(0, src.shape[-1], ATT_DIM):
            x = src[:, g0:g0 + ATT_DIM].astype(F32)
            y = x * cos + pltpu.roll(x, ATT_DIM - quarter, 1) * sa + pltpu.roll(x, quarter, 1) * sb
            dst[:, g0:g0 + ATT_DIM] = (y * scale).astype(dst.dtype)


def rope_qk(z3, tables, att_w, ctx_len):
    bsz, t, _ = z3.shape
    rb = _divisor(math.gcd(ctx_len, t - ctx_len), ROW_BLOCK, 8)
    tbl_spec = pl.BlockSpec((rb, ATT_DIM), lambda b, i: (i, 0))
    out_spec = pl.BlockSpec((None, rb, att_w), lambda b, i: (b, i, 0))
    return pl.pallas_call(
        functools.partial(_rope_kernel, q_scale=ATT_DIM ** -0.5 * math.log2(math.e)),
        grid=(bsz, t // rb),
        in_specs=[pl.BlockSpec((None, rb, att_w), lambda b, i: (b, i, 0)),
                  pl.BlockSpec((None, rb, att_w), lambda b, i: (b, i, 1)),
                  tbl_spec, tbl_spec, tbl_spec],
        out_specs=[out_spec, out_spec],
        out_shape=[jax.ShapeDtypeStruct((bsz, t, att_w), BF16)] * 2,
        compiler_params=_params(("parallel", "parallel")),
        name="rope_qk",
    )(z3, z3, *tables)


def _attn_kernel(lam_ref, g_ref, q_ref, k_ref, v_ref, prev_ref, o_ref, acc1_ref, acc2_ref, *, tk, lam_init):
    del prev_ref
    q = q_ref[...]
    q1, q2 = q[:, :ATT_DIM], q[:, ATT_DIM:]
    lv = lam_ref[...]
    lam = (jnp.exp(jnp.sum(lv[0:1] * lv[1:2], axis=-1, keepdims=True))
           - jnp.exp(jnp.sum(lv[2:3] * lv[3:4], axis=-1, keepdims=True)) + lam_init)
    tq = q.shape[0]
    nt = (((1,), (1,)), ((), ()))
    acc1_ref[...] = jnp.zeros(acc1_ref.shape, F32)
    acc2_ref[...] = jnp.zeros(acc2_ref.shape, F32)

    def one_map(qm, km, vc, acc_ref, m, l):
        s = lax.dot_general(qm, km, nt, preferred_element_type=F32)
        m_new = jnp.maximum(m, jnp.max(s, axis=-1, keepdims=True))
        alpha = jnp.exp2(m - m_new)
        p = jnp.exp2(s - m_new)
        l_new = alpha * l + jnp.sum(p, axis=-1, keepdims=True)
        acc_ref[...] = alpha * acc_ref[...] + jnp.dot(p.astype(BF16), vc, preferred_element_type=F32)
        return m_new, l_new

    def body(c, carry):
        m1, l1, m2, l2 = carry
        r0 = pl.multiple_of(c * tk, tk)
        kc = k_ref[pl.ds(r0, tk), :]
        vc = v_ref[pl.ds(r0, tk), :]
        m1, l1 = one_map(q1, kc[:, :ATT_DIM], vc, acc1_ref, m1, l1)
        m2, l2 = one_map(q2, kc[:, ATT_DIM:], vc, acc2_ref, m2, l2)
        return m1, l1, m2, l2

    neg = jnp.full((tq, 1), -jnp.inf, F32)
    zero = jnp.zeros((tq, 1), F32)
    carry = (neg, zero, neg, zero)
    n_chunks = k_ref.shape[0] // tk
    unroll = min(n_chunks, ATT_MAX_UNROLL)
    peeled = n_chunks % unroll
    for c in range(peeled):
        carry = body(c, carry)

    def trip(i, cr):
        for j in range(unroll):
            cr = body(peeled + unroll * i + j, cr)
        return cr

    _, l1, _, l2 = lax.fori_loop(0, n_chunks // unroll, trip, carry)
    o = acc1_ref[...] / l1 - lam * (acc2_ref[...] / l2)
    o = _rms(o, g_ref[...]) * (1.0 - lam_init)
    o_ref[...] = o.astype(o_ref.dtype)


def diff_attention(qr, kr, z3, lam_vec, att_g, *, v_col_blk, q_row0, n_q, n_kv, lam_init, out, heads):
    bsz, t, att_w = qr.shape
    tq = _divisor(math.gcd(q_row0, n_q) if q_row0 else n_q, ROW_BLOCK, 8)
    tk = _divisor(n_kv, 1024, V7X_LANES)
    qoff = q_row0 // tq
    q_spec = pl.BlockSpec((None, tq, ATT_VDIM), lambda b, h, i: (b, i + qoff, h))
    return pl.pallas_call(
        functools.partial(_attn_kernel, tk=tk, lam_init=lam_init),
        grid=(bsz, heads, n_q // tq),
        in_specs=[pl.BlockSpec(lam_vec.shape, lambda b, h, i: (0, 0)),
                  pl.BlockSpec((1, ATT_VDIM), lambda b, h, i: (0, 0)),
                  q_spec,
                  pl.BlockSpec((None, n_kv, ATT_VDIM), lambda b, h, i: (b, 0, h)),
                  pl.BlockSpec((None, n_kv, ATT_VDIM), lambda b, h, i: (b, 0, v_col_blk + h)),
                  pl.BlockSpec(memory_space=pl.ANY)],
        out_specs=q_spec,
        out_shape=jax.ShapeDtypeStruct((bsz, t, att_w), BF16),
        input_output_aliases={5: 0},
        scratch_shapes=[pltpu.VMEM((tq, ATT_VDIM), F32), pltpu.VMEM((tq, ATT_VDIM), F32)],
        compiler_params=_params(("parallel", "parallel", "arbitrary")),
        name="diff_attention",
    )(lam_vec, att_g.reshape(1, ATT_VDIM), qr, kr, z3, out)


def _conv_kernel(cb_ref, cc_ref, ch_ref, w_ref, o_ref, p_ref, *, ctx_len, rc):
    t, tc = cb_ref.shape
    pad = 8
    p_ref[0:pad, :] = jnp.zeros((pad, tc), F32)
    p_ref[pad + t:pad + t + pad, :] = jnp.zeros((pad, tc), F32)
    for r0 in range(0, t, rc):
        p_ref[pad + r0:pad + r0 + rc, :] = cc_ref[r0:r0 + rc, :].astype(F32) * ch_ref[r0:r0 + rc, :].astype(F32)
    w0, w1, w2 = w_ref[0:1, :], w_ref[1:2, :], w_ref[2:3, :]
    for r0 in range(0, t, rc):
        row = r0 + lax.broadcasted_iota(jnp.int32, (rc, 1), 0)
        prev = p_ref[pad + r0 - 1:pad + r0 - 1 + rc, :]
        cur = p_ref[pad + r0:pad + r0 + rc, :]
        nxt = p_ref[pad + r0 + 1:pad + r0 + 1 + rc, :]
        prev = jnp.where(row == ctx_len, 0.0, prev)
        nxt = jnp.where(row == ctx_len - 1, 0.0, nxt)
        y = cb_ref[r0:r0 + rc, :].astype(F32) * (w0 * prev + w1 * cur + w2 * nxt)
        o_ref[r0:r0 + rc, :] = y.astype(o_ref.dtype)


def short_conv(z3, conv_w, *, cb_off, ctx_len):
    bsz, t, _ = z3.shape
    conv_wd = conv_w.shape[-1]
    tc = V7X_LANES
    rc = _divisor(t, 768, 8)
    blk0 = cb_off // tc
    nblk = conv_wd // tc
    spec = lambda k: pl.BlockSpec((None, t, tc), lambda b, j: (b, 0, blk0 + k * nblk + j))
    return pl.pallas_call(
        functools.partial(_conv_kernel, ctx_len=ctx_len, rc=rc),
        grid=(bsz, nblk),
        in_specs=[spec(0), spec(1), spec(2), pl.BlockSpec((3, tc), lambda b, j: (0, j))],
        out_specs=pl.BlockSpec((None, t, tc), lambda b, j: (b, 0, j)),
        out_shape=jax.ShapeDtypeStruct((bsz, t, conv_wd), BF16),
        scratch_shapes=[pltpu.VMEM((t + 16, tc), F32)],
        compiler_params=_params(("parallel", "parallel")),
        name="short_conv",
    )(z3, z3, z3, conv_w)


def ssm_matrices(a_re, a_im, log_dt, b_re, b_im, c_re, c_im, ssm_d, gp):
    tc = SSM_CHUNK
    n_g, n_p = a_re.shape[1], a_re.shape[2]
    n_i = b_re.shape[-1]
    dt = jnp.exp(log_dt.astype(F32))[..., None]
    lr = jnp.minimum(a_re.astype(F32), -1e-4)
    li = a_im.astype(F32)

    def power(n):
        nn = n.astype(F32)[:, None, None, None]
        mag = jnp.exp(nn * (lr * dt))
        return mag * jnp.cos(nn * (li * dt)), mag * jnp.sin(nn * (li * dt))

    ar, ai = power(jnp.ones((1,), F32))
    ar, ai = ar[0], ai[0]
    den = lr * lr + li * li
    cr = ((ar - 1.0) * lr + ai * li) / den
    ci = (ai * lr - (ar - 1.0) * li) / den
    bre, bim = b_re.astype(F32), b_im.astype(F32)
    br = cr[..., None] * bre - ci[..., None] * bim
    bi = cr[..., None] * bim + ci[..., None] * bre
    cre, cim = c_re.astype(F32), c_im.astype(F32)

    pr, pi = power(jnp.arange(tc + 1))
    wr = pr[..., None] * br - pi[..., None] * bi
    wi = pr[..., None] * bi + pi[..., None] * br
    kern = jnp.einsum('dgip,ndgpj->ndgij', cre, wr) - jnp.einsum('dgip,ndgpj->ndgij', cim, wi)

    r_idx = jnp.arange(tc)[:, None]
    s_idx = jnp.arange(tc)[None, :]

    def toeplitz(lag, d):
        blk = kern[jnp.clip(lag, 0, tc), d]
        blk = jnp.where((lag >= 0)[:, :, None, None, None], blk, 0.0)
        return blk.transpose(2, 0, 4, 1, 3).reshape(n_g, tc * n_i, tc * n_i)

    w = jnp.stack([toeplitz(s_idx - r_idx, 0), toeplitz(r_idx - s_idx, 1)])

    def lane_pad_cols(m):
        q = jnp.arange(n_g) % gp
        onehot = jax.nn.one_hot(q, gp, dtype=F32)
        return (m[:, :, None, :] * onehot[:, None, :, None]).reshape(n_g, m.shape[1], gp * n_p)

    def state_in(d, n_of_r):
        sel = n_of_r
        re = wr[sel, d].transpose(1, 0, 3, 2).reshape(n_g, tc * n_i, n_p)
        im = wi[sel, d].transpose(1, 0, 3, 2).reshape(n_g, tc * n_i, n_p)
        return jnp.stack([lane_pad_cols(re), lane_pad_cols(im)])

    we = jnp.stack([state_in(0, tc - 1 - jnp.arange(tc)), state_in(1, jnp.arange(tc))])

    def state_out(d, n_of_s):
        prs, pis = pr[n_of_s, d], pi[n_of_s, d]
        qr = cre[d][None] * prs[:, :, None, :] - cim[d][None] * pis[:, :, None, :]
        qi = cre[d][None] * pis[:, :, None, :] + cim[d][None] * prs[:, :, None, :]
        fre = qr.transpose(1, 0, 2, 3).reshape(n_g, tc * n_i, n_p)
        fim = -qi.transpose(1, 0, 2, 3).reshape(n_g, tc * n_i, n_p)
        return jnp.stack([lane_pad_cols(fre), lane_pad_cols(fim)]).transpose(0, 1, 3, 2)

    wf = jnp.stack([state_out(0, jnp.arange(tc) + 1), state_out(1, tc - jnp.arange(tc))])

    at = jnp.stack([pr[tc].reshape(2, n_g * n_p), pi[tc].reshape(2, n_g * n_p)], axis=1)
    dd = jnp.tile(ssm_d.astype(F32).reshape(n_g, 1, n_i), (1, 1, tc))
    return w.astype(BF16), we.astype(BF16), wf.astype(BF16), at, dd


def _ssm_state_kernel(u_ref, we_ref, sre_ref, sim_ref):
    gp = u_ref.shape[0]
    for d in range(2):
        for part, dst in ((0, sre_ref), (1, sim_ref)):
            acc = jnp.dot(u_ref[0], we_ref[d, part, 0], preferred_element_type=F32)
            for q in range(1, gp):
                acc = acc + jnp.dot(u_ref[q], we_ref[d, part, q], preferred_element_type=F32)
            dst[d] = acc


def _ssm_scan_kernel(sre_ref, sim_ref, at_ref, hre_ref, him_ref, *, bsz, nch, nctx):
    d = pl.program_id(0)
    atr, ati = at_ref[0:1, :], at_ref[1:2, :]
    zero = jnp.zeros(atr.shape, F32)

    def body(k, carry):
        c_rev = jnp.where(k < nctx, nctx - 1 - k, nch - 1 - (k - nctx))
        c = jnp.where(d == 0, k, c_rev)
        new = []
        for b in range(bsz):
            hr, hi = carry[2 * b], carry[2 * b + 1]
            row = pl.ds(b * nch + c, 1)
            hre_ref[row, :] = hr
            him_ref[row, :] = hi
            new.append(atr * hr - ati * hi + sre_ref[row, :])
            new.append(atr * hi + ati * hr + sim_ref[row, :])
        return tuple(new)

    lax.fori_loop(0, nch, body, tuple([zero] * (2 * bsz)))


def _ssm_out_kernel(u_ref, w_ref, wf_ref, hre_ref, him_ref, dd_ref, y_ref):
    gp = u_ref.shape[0]
    for q in range(gp):
        u = u_ref[q]
        y = u.astype(F32) * dd_ref[q]
        for d in range(2):
            y = y + jnp.dot(u, w_ref[d, q], preferred_element_type=F32)
            y = y + jnp.dot(hre_ref[d].astype(BF16), wf_ref[d, 0, q], preferred_element_type=F32)
            y = y + jnp.dot(him_ref[d].astype(BF16), wf_ref[d, 1, q], preferred_element_type=F32)
        y_ref[q] = y.astype(y_ref.dtype)


def ssm_branch(su, mats, *, ctx_len):
    w, we, wf, at, dd = mats
    bsz, t, width = su.shape
    tc = SSM_CHUNK
    n_g = w.shape[1]
    n_i = width // n_g
    gpp = we.shape[-1]
    n_p = at.shape[-1] // n_g
    gp = gpp // n_p
    nch = t // tc
    mc = bsz * nch
    ug = su.reshape(bsz, nch, tc, n_g, n_i).transpose(3, 0, 1, 2, 4).reshape(n_g, mc, tc * n_i)

    s_spec = pl.BlockSpec((2, mc, gpp), lambda g: (0, 0, g))
    s_shape = jax.ShapeDtypeStruct((2, mc, n_g * n_p), F32)
    sre, sim = pl.pallas_call(
        _ssm_state_kernel,
        grid=(n_g // gp,),
        in_specs=[pl.BlockSpec((gp, mc, tc * n_i), lambda g: (g, 0, 0)),
                  pl.BlockSpec((2, 2, gp, tc * n_i, gpp), lambda g: (0, 0, g, 0, 0))],
        out_specs=[s_spec, s_spec],
        out_shape=[s_shape, s_shape],
        compiler_params=_params(("parallel",)),
        name="ssm_chunk_state",
    )(ug, we)

    lanes = _divisor(n_g * n_p, 4 * V7X_LANES, V7X_LANES)
    st_spec = pl.BlockSpec((None, mc, lanes), lambda di, j: (di, 0, j))
    hre, him = pl.pallas_call(
        functools.partial(_ssm_scan_kernel, bsz=bsz, nch=nch, nctx=ctx_len // tc),
        grid=(2, n_g * n_p // lanes),
        in_specs=[st_spec, st_spec, pl.BlockSpec((None, 2, lanes), lambda di, j: (di, 0, j))],
        out_specs=[st_spec, st_spec],
        out_shape=[s_shape, s_shape],
        compiler_params=_params(("parallel", "parallel")),
        name="ssm_chunk_scan",
    )(sre, sim, at)

    yg = pl.pallas_call(
        _ssm_out_kernel,
        grid=(n_g // gp,),
        in_specs=[pl.BlockSpec((gp, mc, tc * n_i), lambda g: (g, 0, 0)),
                  pl.BlockSpec((2, gp, tc * n_i, tc * n_i), lambda g: (0, g, 0, 0)),
                  pl.BlockSpec((2, 2, gp, gpp, tc * n_i), lambda g: (0, 0, g, 0, 0)),
                  s_spec, s_spec,
                  pl.BlockSpec((gp, 1, tc * n_i), lambda g: (g, 0, 0))],
        out_specs=pl.BlockSpec((gp, mc, tc * n_i), lambda g: (g, 0, 0)),
        out_shape=jax.ShapeDtypeStruct((n_g, mc, tc * n_i), BF16),
        compiler_params=_params(("parallel",)),
        name="ssm_chunk_out",
    )(ug, w, wf, hre, him, dd)
    return yg.reshape(n_g, bsz, nch, tc, n_i).transpose(1, 2, 3, 0, 4).reshape(bsz * t, width)


def _glu_kernel(y_ref, w_ref, b_ref, o_ref):
    y = y_ref[...].astype(F32)
    g = 0.5 * y * (1.0 + jnp.tanh(math.sqrt(2.0 / math.pi) * (y + 0.044715 * (y * y * y))))
    r = jnp.dot(g.astype(BF16), w_ref[...], preferred_element_type=F32) + b_ref[...]
    o_ref[...] = (g * jax.nn.sigmoid(r)).astype(o_ref.dtype)


def s5_glu(y, w, b, layer):
    m, width = y.shape
    tm = _divisor(m, 768, 16)
    return pl.pallas_call(
        _glu_kernel,
        grid=(m // tm,),
        in_specs=[pl.BlockSpec((tm, width), lambda i: (i, 0)),
                  pl.BlockSpec((None, width, width), lambda i: (layer, 0, 0)),
                  pl.BlockSpec((1, width), lambda i: (0, 0))],
        out_specs=pl.BlockSpec((tm, width), lambda i: (i, 0)),
        out_shape=jax.ShapeDtypeStruct((m, width), BF16),
        compiler_params=_params(("parallel",)),
        name="s5_glu",
    )(y, w, b.reshape(1, width))


def _merge_kernel(att_ref, conv_ref, ssm_ref, g0_ref, g1_ref, g2_ref, pa_ref, pc_ref, ps_ref, o_ref):
    def branch(x_ref, p_ref, g_ref):
        y = jnp.dot(x_ref[...], p_ref[...], preferred_element_type=F32)
        return jax.nn.sigmoid(g_ref[...].astype(F32)) * y

    m = branch(att_ref, pa_ref, g0_ref) + branch(conv_ref, pc_ref, g1_ref) + branch(ssm_ref, ps_ref, g2_ref)
    o_ref[...] = m.astype(o_ref.dtype)


def branch_merge(att, conv, ssm, z, p_att, p_conv, p_ssm, layer, *, g_off):
    m, d = att.shape[0], p_att.shape[2]
    tm = _divisor(m, 768, 16)
    tn = _divisor(math.gcd(d, g_off), 1024, V7X_LANES)
    gblk = g_off // tn
    nj = d // tn
    row = lambda a: pl.BlockSpec((tm, a.shape[1]), lambda i, j: (i, 0))
    gate = lambda k: pl.BlockSpec((tm, tn), lambda i, j: (i, gblk + k * nj + j))
    col = lambda p: pl.BlockSpec((None, p.shape[1], tn), lambda i, j: (layer, 0, j))
    return pl.pallas_call(
        _merge_kernel,
        grid=(m // tm, nj),
        in_specs=[row(att), row(conv), row(ssm), gate(0), gate(1), gate(2), col(p_att), col(p_conv), col(p_ssm)],
        out_specs=pl.BlockSpec((tm, tn), lambda i, j: (i, j)),
        out_shape=jax.ShapeDtypeStruct((m, d), BF16),
        compiler_params=_params(("parallel", "parallel")),
        name="branch_merge",
    )(att, conv, ssm, z, z, z, p_att, p_conv, p_ssm)


def _first_max(x, axis, iota):
    mx = jnp.max(x, axis=axis, keepdims=True)
    n = x.shape[axis]
    idx = jnp.min(jnp.where(x == mx, iota, n), axis=axis, keepdims=True)
    return mx, idx, iota == idx


def _router_kernel(lg_ref, b_ref, idx_ref, wgt_ref):
    n_e, tm = lg_ref.shape
    per = n_e // N_EXPERT_GROUPS
    scores = jax.nn.sigmoid(lg_ref[...])
    sel = scores + b_ref[...]
    neg = -jnp.inf
    grp = sel.reshape(N_EXPERT_GROUPS, per, tm)
    iota_e = lax.broadcasted_iota(jnp.int32, grp.shape, 1)
    m1, _, first = _first_max(grp, 1, iota_e)
    m2 = jnp.max(jnp.where(first, neg, grp), axis=1, keepdims=True)
    gscore = m1 + m2
    iota_g = lax.broadcasted_iota(jnp.int32, gscore.shape, 0)
    gmask = jnp.zeros(gscore.shape, F32)
    for _ in range(TOPK_GROUPS):
        _, _, hit = _first_max(gscore, 0, iota_g)
        gmask = jnp.where(hit, 1.0, gmask)
        gscore = jnp.where(hit, neg, gscore)
    cand = jnp.where(jnp.broadcast_to(gmask, grp.shape) > 0.0, grp, neg).reshape(n_e, tm)
    iota_x = lax.broadcasted_iota(jnp.int32, cand.shape, 0)
    chosen = jnp.zeros(cand.shape, F32)
    for r in range(TOP_K):
        _, idx, hit = _first_max(cand, 0, iota_x)
        chosen = jnp.where(hit, 1.0, chosen)
        cand = jnp.where(hit, neg, cand)
        idx_ref[r:r + 1, :] = idx
        wgt_ref[r:r + 1, :] = jnp.sum(jnp.where(hit, scores, 0.0), axis=0, keepdims=True)
    norm = ROUTED_SCALE / jnp.sum(chosen * scores, axis=0, keepdims=True)
    wgt_ref[...] = wgt_ref[...] * norm


def route(logits_t, router_b):
    n_e, m = logits_t.shape
    tm = _divisor(m, 512, V7X_LANES)
    spec = lambda r: pl.BlockSpec((r, tm), lambda i: (0, i))
    return pl.pallas_call(
        _router_kernel,
        grid=(m // tm,),
        in_specs=[spec(n_e), pl.BlockSpec((n_e, 1), lambda i: (0, 0))],
        out_specs=[spec(TOP_K), spec(TOP_K)],
        out_shape=[jax.ShapeDtypeStruct((TOP_K, m), jnp.int32), jax.ShapeDtypeStruct((TOP_K, m), F32)],
        compiler_params=_params(("parallel",)),
        name="moe_route",
    )(logits_t, router_b.reshape(n_e, 1))


MOE_TILE = 256


def _rank_kernel(eidx_ref, rank_ref, cnt_ref):
    @pl.when(pl.program_id(0) == 0)
    def _():
        cnt_ref[...] = jnp.zeros(cnt_ref.shape, F32)

    n_e = cnt_ref.shape[0]
    tm = eidx_ref.shape[1]
    earlier = (lax.broadcasted_iota(jnp.int32, (tm, tm), 0) < lax.broadcasted_iota(jnp.int32, (tm, tm), 1))
    upper = jnp.where(earlier, 1.0, 0.0).astype(BF16)
    iota_e = lax.broadcasted_iota(jnp.int32, (n_e, tm), 0)
    base = cnt_ref[...]
    for r in range(TOP_K):
        onehot = jnp.where(iota_e == eidx_ref[r:r + 1, :], 1.0, 0.0)
        before = jnp.dot(onehot.astype(BF16), upper, preferred_element_type=F32)
        rank_ref[r:r + 1, :] = jnp.sum(onehot * (base + before), axis=0, keepdims=True).astype(jnp.int32)
        base = base + jnp.sum(onehot, axis=1, keepdims=True)
    cnt_ref[...] = base


def moe_rank(eidx, n_e):
    _, m = eidx.shape
    tm = _divisor(m, 256, V7X_LANES)
    return pl.pallas_call(
        _rank_kernel,
        grid=(m // tm,),
        in_specs=[pl.BlockSpec((TOP_K, tm), lambda i: (0, i))],
        out_specs=[pl.BlockSpec((TOP_K, tm), lambda i: (0, i)), pl.BlockSpec((n_e, 1), lambda i: (0, 0))],
        out_shape=[jax.ShapeDtypeStruct((TOP_K, m), jnp.int32), jax.ShapeDtypeStruct((n_e, 1), F32)],
        compiler_params=_params(("arbitrary",)),
        name="moe_rank",
    )(eidx)


def _foreach(lo, hi, fn):
    def body(j, carry):
        fn(j)
        return carry

    lax.fori_loop(lo, hi, body, 0)


def _dispatch_kernel(pad_ref, pos_ref, x_ref, xs_ref, zero_ref, sem, zsem, *, n_e):
    tm = x_ref.shape[0]
    tr = zero_ref.shape[0]
    n_tiles = xs_ref.shape[0] // tr

    def zero_row_copy(row):
        return pltpu.make_async_copy(zero_ref.at[pl.ds(0, 1)], xs_ref.at[pl.ds(row, 1)], zsem)

    def zero_tile_copy(tile):
        return pltpu.make_async_copy(zero_ref, xs_ref.at[pl.ds(tile * tr, tr)], zsem)

    @pl.when(pl.program_id(0) == 0)
    def _():
        zero_ref[...] = jnp.zeros(zero_ref.shape, zero_ref.dtype)

        first_unused = pad_ref[2 * n_e]
        _foreach(0, n_e, lambda e: _foreach(0, pad_ref[n_e + e], lambda j: zero_row_copy(pad_ref[e] + j).start()))
        _foreach(first_unused, n_tiles, lambda t: zero_tile_copy(t).start())
        _foreach(0, n_e, lambda e: _foreach(0, pad_ref[n_e + e], lambda j: zero_row_copy(0).wait()))
        _foreach(first_unused, n_tiles, lambda t: zero_tile_copy(0).wait())

    def scatter_token(n):
        for r in range(TOP_K):
            pltpu.make_async_copy(x_ref.at[pl.ds(n, 1)], xs_ref.at[pl.ds(pos_ref[r, n], 1)], sem).start()

    _foreach(0, tm, scatter_token)
    for r in range(TOP_K):
        pltpu.make_async_copy(x_ref, xs_ref.at[pl.ds(0, tm)], sem).wait()


def moe_dispatch(hp, pos, pad_info, n_rows, n_e):
    m, w = hp.shape
    tm = _divisor(m, 256, V7X_LANES)
    grid_spec = pltpu.PrefetchScalarGridSpec(
        num_scalar_prefetch=1,
        grid=(m // tm,),
        in_specs=[pl.BlockSpec((TOP_K, tm), lambda i, pad: (0, i), memory_space=pltpu.SMEM),
                  pl.BlockSpec((tm, w), lambda i, pad: (i, 0))],
        out_specs=pl.BlockSpec(memory_space=pl.ANY),
        scratch_shapes=[pltpu.VMEM((MOE_TILE, w), jnp.uint32), pltpu.SemaphoreType.DMA(()),
                        pltpu.SemaphoreType.DMA(())],
    )
    return pl.pallas_call(
        functools.partial(_dispatch_kernel, n_e=n_e),
        grid_spec=grid_spec,
        out_shape=jax.ShapeDtypeStruct((n_rows, w), jnp.uint32),
        compiler_params=_params(("arbitrary",)),
        name="moe_dispatch",
    )(pad_info, pos, hp)


def _moe_ffn_kernel(te_ref, nv_ref, xs_ref, wg_ref, wu_ref, wd_ref, ys_ref, wgb_ref, wub_ref, wdb_ref):
    i = pl.program_id(0)
    nv = nv_ref[i]
    new_expert = jnp.logical_or(i == 0, te_ref[i] != te_ref[jnp.maximum(i - 1, 0)])

    @pl.when(jnp.logical_and(new_expert, nv > 0))
    def _():
        wgb_ref[...] = wg_ref[...].astype(BF16)
        wub_ref[...] = wu_ref[...].astype(BF16)
        wdb_ref[...] = wd_ref[...].astype(BF16)

    @pl.when(nv > 0)
    def _():
        rows = lax.broadcasted_iota(jnp.int32, (xs_ref.shape[0], 1), 0)
        packed = jnp.where(rows < nv, xs_ref[...], jnp.uint32(0))
        lo, hi = _unpack_halves(packed)
        x = jnp.concatenate([lo.astype(BF16), hi.astype(BF16)], axis=1)
        g = jnp.dot(x, wgb_ref[...], preferred_element_type=F32)
        u = jnp.dot(x, wub_ref[...], preferred_element_type=F32)
        hid = (g * jax.nn.sigmoid(g) * u).astype(BF16)
        y = jnp.dot(hid, wdb_ref[...], preferred_element_type=F32)
        half = y.shape[1] // 2
        ys_ref[...] = _pack_halves(y[:, :half], y[:, half:])

    @pl.when(nv == 0)
    def _():
        ys_ref[...] = jnp.zeros(ys_ref.shape, jnp.uint32)


def moe_ffn(xs, tile_expert, tile_rows, w_gate, w_up, w_down, layer):
    n_rows, w = xs.shape
    _, _, d, ff = w_gate.shape
    tr = MOE_TILE
    grid_spec = pltpu.PrefetchScalarGridSpec(
        num_scalar_prefetch=2,
        grid=(n_rows // tr,),
        in_specs=[pl.BlockSpec((tr, w), lambda i, te, nv: (i, 0)),
                  pl.BlockSpec((None, None, d, ff), lambda i, te, nv: (layer, te[i], 0, 0)),
                  pl.BlockSpec((None, None, d, ff), lambda i, te, nv: (layer, te[i], 0, 0)),
                  pl.BlockSpec((None, None, ff, d), lambda i, te, nv: (layer, te[i], 0, 0))],
        out_specs=pl.BlockSpec((tr, w), lambda i, te, nv: (i, 0)),
        scratch_shapes=[pltpu.VMEM((d, ff), BF16), pltpu.VMEM((d, ff), BF16), pltpu.VMEM((ff, d), BF16)],
    )
    return pl.pallas_call(
        _moe_ffn_kernel,
        grid_spec=grid_spec,
        out_shape=jax.ShapeDtypeStruct((n_rows, w), jnp.uint32),
        compiler_params=_params(("arbitrary",)),
        name="moe_ffn",
    )(tile_expert, tile_rows, xs, w_gate, w_up, w_down)


def _combine_kernel(pos_ref, w_ref, sh_ref, ys_ref, o_ref, buf_ref, sem, *, cw):
    tm, half = buf_ref.shape[1], buf_ref.shape[2]

    def gather_token(n):
        for r in range(TOP_K):
            pltpu.make_async_copy(ys_ref.at[pl.ds(pos_ref[r, n], 1)], buf_ref.at[r, pl.ds(n, 1)], sem).start()

    _foreach(0, tm, gather_token)
    for r in range(TOP_K):
        pltpu.make_async_copy(ys_ref.at[pl.ds(0, tm)], buf_ref.at[r], sem).wait()
    for c0 in range(0, half, cw):
        acc_lo = sh_ref[:, c0:c0 + cw].astype(F32)
        acc_hi = sh_ref[:, half + c0:half + c0 + cw].astype(F32)
        for r in range(TOP_K):
            lo, hi = _unpack_halves(buf_ref[r, :, c0:c0 + cw])
            wr = w_ref[:, r:r + 1]
            acc_lo = acc_lo + wr * lo
            acc_hi = acc_hi + wr * hi
        o_ref[:, c0:c0 + cw] = acc_lo.astype(o_ref.dtype)
        o_ref[:, half + c0:half + c0 + cw] = acc_hi.astype(o_ref.dtype)


def moe_combine(ys, pos, wgt_t, shared):
    m, d = shared.shape
    w = ys.shape[1]
    tm = _divisor(m, 128, V7X_LANES)
    return pl.pallas_call(
        functools.partial(_combine_kernel, cw=_divisor(w, 256, V7X_LANES)),
        grid=(m // tm,),
        in_specs=[pl.BlockSpec((TOP_K, tm), lambda i: (0, i), memory_space=pltpu.SMEM),
                  pl.BlockSpec((tm, TOP_K), lambda i: (i, 0)),
                  pl.BlockSpec((tm, d), lambda i: (i, 0)),
                  pl.BlockSpec(memory_space=pl.ANY)],
        out_specs=pl.BlockSpec((tm, d), lambda i: (i, 0)),
        out_shape=jax.ShapeDtypeStruct((m, d), BF16),
        scratch_shapes=[pltpu.VMEM((TOP_K, tm, w), jnp.uint32), pltpu.SemaphoreType.DMA(())],
        compiler_params=_params(("arbitrary",)),
        name="moe_combine",
    )(pos, wgt_t, shared, ys)


def _ffn_up_kernel(x_ref, wg_ref, wu_ref, o_ref):
    x = x_ref[...]
    g = jnp.dot(x, wg_ref[...], preferred_element_type=F32)
    u = jnp.dot(x, wu_ref[...], preferred_element_type=F32)
    o_ref[...] = (g * jax.nn.sigmoid(g) * u).astype(o_ref.dtype)


def ffn_up(h, w_gate, w_up, layer):
    m, d = h.shape
    ff = w_gate.shape[2]
    tm = _divisor(m, 768, 16)
    wspec = pl.BlockSpec((None, d, ff), lambda i: (layer, 0, 0))
    return pl.pallas_call(
        _ffn_up_kernel,
        grid=(m // tm,),
        in_specs=[pl.BlockSpec((tm, d), lambda i: (i, 0)), wspec, wspec],
        out_specs=pl.BlockSpec((tm, ff), lambda i: (i, 0)),
        out_shape=jax.ShapeDtypeStruct((m, ff), BF16),
        compiler_params=_params(("parallel",)),
        name="ffn_up",
    )(h, w_gate, w_up)


def _pos_kernel(eidx_ref, rank_ref, start_ref, pos_ref):
    iota_e = lax.broadcasted_iota(jnp.int32, (start_ref.shape[0], eidx_ref.shape[1]), 0)
    for r in range(TOP_K):
        first_row = jnp.sum(jnp.where(iota_e == eidx_ref[r:r + 1, :], start_ref[...], 0), axis=0, keepdims=True)
        pos_ref[r:r + 1, :] = rank_ref[r:r + 1, :] + first_row


def moe_pos(eidx, rank, row_start):
    _, m = eidx.shape
    n_e = row_start.shape[0]
    tm = _divisor(m, 512, V7X_LANES)
    spec = pl.BlockSpec((TOP_K, tm), lambda i: (0, i))
    return pl.pallas_call(
        _pos_kernel,
        grid=(m // tm,),
        in_specs=[spec, spec, pl.BlockSpec((n_e, 1), lambda i: (0, 0))],
        out_specs=spec,
        out_shape=jax.ShapeDtypeStruct((TOP_K, m), jnp.int32),
        compiler_params=_params(("parallel",)),
        name="moe_pos",
    )(eidx, rank, row_start.reshape(n_e, 1))


def moe_sparse(h2, h2p, eidx, wgt, w_gate, w_up, w_down, ws_gate, ws_up, ws_down, layer):
    m, d = h2.shape
    n_e = w_gate.shape[1]
    tr = MOE_TILE
    rank, cnt = moe_rank(eidx, n_e)
    counts = cnt[:, 0].astype(jnp.int32)
    tiles_per = (counts + (tr - 1)) // tr
    tile_end = jnp.cumsum(tiles_per)
    tile_start = tile_end - tiles_per
    pos = moe_pos(eidx, rank, tile_start * tr)
    n_tiles = (TOP_K * m) // tr + n_e
    t_idx = jnp.arange(n_tiles, dtype=jnp.int32)
    tile_expert = jnp.minimum(jnp.sum((tile_end[None, :] <= t_idx[:, None]).astype(jnp.int32), axis=1), n_e - 1)
    owner = tile_expert[:, None] == jnp.arange(n_e, dtype=jnp.int32)[None, :]
    left = jnp.sum(jnp.where(owner, counts[None, :] - (t_idx[:, None] - tile_start[None, :]) * tr, 0), axis=1)
    tile_rows = jnp.clip(left, 0, tr).astype(jnp.int32)
    pad_info = jnp.concatenate([tile_start * tr + counts, tiles_per * tr - counts, tile_end[-1:]]).astype(jnp.int32)
    xs = moe_dispatch(h2p, pos, pad_info, n_tiles * tr, n_e)
    ys = moe_ffn(xs, tile_expert, tile_rows, w_gate, w_up, w_down, layer)
    shared = matmul(ffn_up(h2, ws_gate, ws_up, layer), ws_down, layer)
    return moe_combine(ys, pos, wgt.T, shared)


def _router_split(router_w):
    rw_t = router_w.T
    rw_hi = rw_t.astype(BF16)
    rw_lo = (rw_t - rw_hi.astype(F32)).astype(BF16)
    return rw_hi, rw_lo


def kernel(x, c, ctx, c_ctx, ada_w, ada_b, norm_g, w_in, lam_vec, att_g, conv_w,
           ssm_a_re, ssm_a_im, ssm_log_dt, ssm_b_re, ssm_b_im, ssm_c_re, ssm_c_im, ssm_d,
           glu_w, glu_b, p_att, p_conv, p_ssm, w_o, router_w, router_b,
           exp_w_gate, exp_w_up, exp_w_down, sh_w_gate, sh_w_up, sh_w_down):
    bsz, n_lat, d = x.shape
    n_ctx = ctx.shape[1]
    t = n_ctx + n_lat
    m = bsz * t
    depth = ada_w.shape[0]
    att_w = p_att.shape[1]
    heads = att_w // ATT_VDIM
    conv_wd = conv_w.shape[-1]
    ssm_wd = ssm_d.shape[-1]
    n_e = router_w.shape[-1]
    n_p = ssm_a_re.shape[-1]
    assert bsz + 1 <= 8 and n_ctx % SSM_CHUNK == 0 and n_lat % SSM_CHUNK == 0 and n_lat % GRID_W == 0
    assert V7X_LANES % n_p == 0 and n_e % N_EXPERT_GROUPS == 0
    k_off = att_w
    v_off = 2 * att_w
    cb_off = 3 * att_w
    su_off = cb_off + 3 * conv_wd
    g_off = su_off + ssm_wd

    stream = jnp.concatenate([ctx, x], axis=1)
    cond_cols = jnp.zeros((d, 8), F32).at[:, :bsz].set(c.T).at[:, bsz].set(c_ctx)
    tables = rope_tables(n_ctx, n_lat)
    w_in_b, p_att_b, p_conv_b, p_ssm_b, w_o_b, glu_w_b, ws_gate_b, ws_up_b, ws_down_b = (
        a.astype(BF16) for a in (w_in, p_att, p_conv, p_ssm, w_o, glu_w, sh_w_gate, sh_w_up, sh_w_down))

    pending = None
    for i in range(depth):
        lam_init = 0.8 - 0.6 * math.exp(-0.3 * i)
        mod8 = modulation(cond_cols, ada_w, ada_b, bsz + 1, i).reshape(8, N_MOD, d)
        mod = jnp.stack([jnp.broadcast_to(mod8[bsz], (bsz, N_MOD, d)), mod8[:bsz]], axis=1)
        gains = norm_g[i]

        if pending is None:
            (h,) = resid_adaln(stream, None, mod, gains, ctx_len=n_ctx, pre_idx=0, shift_idx=0, scale_idx=1)
        else:
            branch, p_mod, p_gains = pending
            stream, h = resid_adaln(stream, branch, jnp.concatenate([mod, p_mod], axis=2),
                                    jnp.concatenate([gains, p_gains], axis=0), ctx_len=n_ctx,
                                    post_idx=4 + 3, gate_idx=N_MOD + 5, pre_idx=0, shift_idx=0, scale_idx=1)
        z = matmul(h.reshape(m, d), w_in_b, i)
        z3 = z.reshape(bsz, t, -1)

        qr, kr = rope_qk(z3, tables, att_w, n_ctx)
        att = jnp.zeros((bsz, t, att_w), BF16)
        att = diff_attention(qr, kr, z3, lam_vec[i], att_g[i], v_col_blk=v_off // ATT_VDIM, q_row0=0, n_q=n_ctx,
                             n_kv=n_ctx, lam_init=lam_init, out=att, heads=heads)
        att = diff_attention(qr, kr, z3, lam_vec[i], att_g[i], v_col_blk=v_off // ATT_VDIM, q_row0=n_ctx, n_q=n_lat,
                             n_kv=t, lam_init=lam_init, out=att, heads=heads)

        conv = short_conv(z3, conv_w[i], cb_off=cb_off, ctx_len=n_ctx)

        mats = ssm_matrices(ssm_a_re[i], ssm_a_im[i], ssm_log_dt[i], ssm_b_re[i], ssm_b_im[i],
                            ssm_c_re[i], ssm_c_im[i], ssm_d[i], V7X_LANES // n_p)
        y = ssm_branch(z3[:, :, su_off:su_off + ssm_wd], mats, ctx_len=n_ctx)
        ssm = s5_glu(y, glu_w_b, glu_b[i], i)

        merged = branch_merge(att.reshape(m, att_w), conv.reshape(m, conv_wd), ssm, z,
                              p_att_b, p_conv_b, p_ssm_b, i, g_off=g_off)
        o = matmul(merged, w_o_b, i).reshape(bsz, t, d)

        stream, h2, h2p, logits_t = resid_adaln(stream, o, mod, gains, ctx_len=n_ctx, post_idx=1, gate_idx=2,
                                                pre_idx=2, shift_idx=3, scale_idx=4,
                                                router=_router_split(router_w[i]))
        eidx, wgt = route(logits_t, router_b[i])
        f = moe_sparse(h2.reshape(m, d), h2p.reshape(m, d // 2), eidx, wgt, exp_w_gate, exp_w_up, exp_w_down,
                       ws_gate_b, ws_up_b, ws_down_b, i).reshape(bsz, t, d)
        pending = (f, mod, gains)

    branch, p_mod, p_gains = pending
    (out,) = resid_adaln(stream, branch, p_mod, p_gains, ctx_len=n_ctx, post_idx=3, gate_idx=5,
                         latent_only_out=True)
    return out
```

```python
import functools
import math

import jax
import jax.numpy as jnp
from jax import lax
from jax.experimental import pallas as pl
from jax.experimental.pallas import tpu as pltpu

F32 = jnp.float32
BF16 = jnp.bfloat16

GRID_W = 64
ROPE_BASE = 10000.0
ATT_DIM = 128
ATT_VDIM = 2 * ATT_DIM
N_BRANCH = 3
TOP_K = 8
N_EXPERT_GROUPS = 8
TOPK_GROUPS = 4
ROUTED_SCALE = 2.5
N_MOD = 6
EPS = 1e-6
SSM_CHUNK = 16
ATT_MAX_UNROLL = 16
SSM_LAYOUT_ROWS = 8448

V7X_LANES = 128
V7X_VMEM_LIMIT_BYTES = 56 * 1024 * 1024
ROW_BLOCK = 256


def _divisor(n, target, mult):
    best = None
    for d in range(mult, min(n, target) + 1, mult):
        if n % d == 0:
            best = d
    return best if best is not None else n


def _params(sem):
    return pltpu.CompilerParams(dimension_semantics=sem, vmem_limit_bytes=V7X_VMEM_LIMIT_BYTES)


def _mod_kernel(s_ref, w_ref, b_ref, o_ref, *, rows, kc):
    s = s_ref[...]
    s = s * jax.nn.sigmoid(s)
    o_ref[...] = jnp.zeros(o_ref.shape, F32)
    d = w_ref.shape[0]
    for r in range(rows):
        acc = b_ref[...]
        for k0 in range(0, d, kc):
            acc = acc + jnp.sum(w_ref[k0:k0 + kc, :] * s[k0:k0 + kc, r:r + 1], axis=0, keepdims=True)
        o_ref[r:r + 1, :] = acc


def modulation(cond_cols, w, b, rows, layer):
    n_l, d, n = w.shape
    tn = _divisor(n, 512, V7X_LANES)
    kc = _divisor(d, 512, 8)
    return pl.pallas_call(
        functools.partial(_mod_kernel, rows=rows, kc=kc),
        grid=(n // tn,),
        in_specs=[pl.BlockSpec((d, 8), lambda j: (0, 0)),
                  pl.BlockSpec((None, d, tn), lambda j: (layer, 0, j)),
                  pl.BlockSpec((None, 1, tn), lambda j: (layer, 0, j))],
        out_specs=pl.BlockSpec((8, tn), lambda j: (0, j)),
        out_shape=jax.ShapeDtypeStruct((8, n), F32),
        compiler_params=_params(("arbitrary",)),
        name="modulation",
    )(cond_cols, w, b.reshape(n_l, 1, n))


def _rms(x, g):
    return x * lax.rsqrt(jnp.mean(x * x, axis=-1, keepdims=True) + EPS) * g


def _pack_halves(a, b):
    lo = lax.bitcast_convert_type(a.astype(BF16).astype(F32), jnp.uint32) >> 16
    hi = lax.bitcast_convert_type(b.astype(BF16).astype(F32), jnp.uint32) & jnp.uint32(0xFFFF0000)
    return lo | hi


def _unpack_halves(p):
    lo = lax.bitcast_convert_type(p << 16, F32)
    hi = lax.bitcast_convert_type(p & jnp.uint32(0xFFFF0000), F32)
    return lo, hi


def _resid_adaln_kernel(*refs, has_branch, has_h, has_router, post_idx, gate_idx, pre_idx, shift_idx, scale_idx):
    it = iter(refs)
    x_ref = next(it)
    o_ref = next(it) if has_branch else None
    mod_ref = next(it)
    g_ref = next(it)
    rwh_ref = next(it) if has_router else None
    rwl_ref = next(it) if has_router else None
    xout_ref = next(it) if has_branch else None
    h_ref = next(it) if has_h else None
    hp_ref = next(it) if has_router else None
    lg_ref = next(it) if has_router else None

    x = x_ref[...]
    if has_branch:
        o = o_ref[...].astype(F32)
        x = x + mod_ref[gate_idx:gate_idx + 1, :] * _rms(o, g_ref[post_idx:post_idx + 1, :])
        xout_ref[...] = x
    if has_h:
        h = _rms(x, g_ref[pre_idx:pre_idx + 1, :])
        h = h * (1.0 + mod_ref[scale_idx:scale_idx + 1, :]) + mod_ref[shift_idx:shift_idx + 1, :]
        h_ref[...] = h.astype(BF16)
        if has_router:
            h_hi = h.astype(BF16)
            h_lo = (h - h_hi.astype(F32)).astype(BF16)
            nt = (((1,), (1,)), ((), ()))
            lg = lax.dot_general(rwh_ref[...], h_hi, nt, preferred_element_type=F32)
            lg = lg + lax.dot_general(rwh_ref[...], h_lo, nt, preferred_element_type=F32)
            lg = lg + lax.dot_general(rwl_ref[...], h_hi, nt, preferred_element_type=F32)
            lg_ref[...] = lg
            half = h.shape[1] // 2
            hp_ref[...] = _pack_halves(h[:, :half], h[:, half:])


def resid_adaln(x, branch, mod, gains, *, ctx_len, post_idx=0, gate_idx=0, pre_idx=None,
                shift_idx=0, scale_idx=0, router=None, latent_only_out=False):
    bsz, t, d = x.shape
    rb = _divisor(math.gcd(ctx_len, t - ctx_len), ROW_BLOCK, 8)
    nctx = ctx_len // rb
    has_branch = branch is not None
    has_h = pre_idx is not None
    has_router = router is not None
    off = nctx if latent_only_out else 0
    nblk = t // rb - off

    def row_map(b, i):
        return (b, i + off, 0)

    in_specs = [pl.BlockSpec((None, rb, d), row_map)]
    args = [x]
    if has_branch:
        in_specs.append(pl.BlockSpec((None, rb, d), row_map))
        args.append(branch)
    in_specs.append(pl.BlockSpec((None, None, mod.shape[2], d),
                                 lambda b, i: (b, jnp.where(i + off >= nctx, 1, 0), 0, 0)))
    args.append(mod)
    in_specs.append(pl.BlockSpec(gains.shape, lambda b, i: (0, 0)))
    args.append(gains)
    if has_router:
        rw_hi, rw_lo = router
        in_specs += [pl.BlockSpec(rw_hi.shape, lambda b, i: (0, 0)), pl.BlockSpec(rw_lo.shape, lambda b, i: (0, 0))]
        args += [rw_hi, rw_lo]
    out_specs, out_shape = [], []
    if has_branch:
        out_specs.append(pl.BlockSpec((None, rb, d), lambda b, i: (b, i, 0)))
        out_shape.append(jax.ShapeDtypeStruct((bsz, nblk * rb, d), F32))
    if has_h:
        out_specs.append(pl.BlockSpec((None, rb, d), lambda b, i: (b, i, 0)))
        out_shape.append(jax.ShapeDtypeStruct((bsz, t, d), BF16))
    if has_router:
        n_e = router[0].shape[0]
        out_specs.append(pl.BlockSpec((None, rb, d // 2), lambda b, i: (b, i, 0)))
        out_shape.append(jax.ShapeDtypeStruct((bsz, t, d // 2), jnp.uint32))
        out_specs.append(pl.BlockSpec((n_e, rb), lambda b, i: (0, b * nblk + i)))
        out_shape.append(jax.ShapeDtypeStruct((n_e, bsz * t), F32))
    return pl.pallas_call(
        functools.partial(_resid_adaln_kernel, has_branch=has_branch, has_h=has_h, has_router=has_router,
                          post_idx=post_idx, gate_idx=gate_idx, pre_idx=pre_idx, shift_idx=shift_idx,
                          scale_idx=scale_idx),
        grid=(bsz, nblk),
        in_specs=in_specs,
        out_specs=out_specs,
        out_shape=out_shape,
        compiler_params=_params(("parallel", "parallel")),
        name="resid_adaln",
    )(*args)


def _mm_kernel(a_ref, b_ref, o_ref):
    o_ref[...] = jnp.dot(a_ref[...], b_ref[...], preferred_element_type=F32).astype(o_ref.dtype)


def matmul(a, b, layer, out_dtype=BF16, tm_target=768, tn_target=1024):
    m, k = a.shape
    _, _, n = b.shape
    tm = _divisor(m, tm_target, 16)
    tn = _divisor(n, tn_target, V7X_LANES)
    return pl.pallas_call(
        _mm_kernel,
        grid=(m // tm, n // tn),
        in_specs=[pl.BlockSpec((tm, k), lambda i, j: (i, 0)),
                  pl.BlockSpec((None, k, tn), lambda i, j: (layer, 0, j))],
        out_specs=pl.BlockSpec((tm, tn), lambda i, j: (i, j)),
        out_shape=jax.ShapeDtypeStruct((m, n), out_dtype),
        compiler_params=_params(("parallel", "parallel")),
        name="matmul",
    )(a, b)


def rope_tables(ctx_len, n_lat):
    nf = ATT_DIM // 4
    rows = n_lat // GRID_W
    row = jnp.repeat(jnp.arange(rows), GRID_W).astype(F32)
    col = jnp.tile(jnp.arange(GRID_W), rows).astype(F32)
    inv = ROPE_BASE ** (-jnp.arange(nf, dtype=F32) / nf)
    ang_r = row[:, None] * inv
    ang_c = col[:, None] * inv
    zero = jnp.zeros_like(ang_r)
    cos = jnp.concatenate([jnp.cos(ang_r), jnp.cos(ang_r), jnp.cos(ang_c), jnp.cos(ang_c)], axis=-1)
    sa = jnp.concatenate([-jnp.sin(ang_r), zero, -jnp.sin(ang_c), zero], axis=-1)
    sb = jnp.concatenate([zero, jnp.sin(ang_r), zero, jnp.sin(ang_c)], axis=-1)
    pad = lambda tbl, v: jnp.concatenate([jnp.full((ctx_len, ATT_DIM), v, F32), tbl], axis=0)
    return pad(cos, 1.0), pad(sa, 0.0), pad(sb, 0.0)


def _rope_kernel(q_ref, k_ref, cos_ref, sa_ref, sb_ref, qo_ref, ko_ref, *, q_scale):
    cos, sa, sb = cos_ref[...], sa_ref[...], sb_ref[...]
    quarter = ATT_DIM // 4
    for src, dst, scale in ((q_ref, qo_ref, q_scale), (k_ref, ko_ref, 1.0)):
        for g0 in range(0, src.shape[-1], ATT_DIM):
            x = src[:, g0:g0 + ATT_DIM].astype(F32)
            y = x * cos + pltpu.roll(x, ATT_DIM - quarter, 1) * sa + pltpu.roll(x, quarter, 1) * sb
            dst[:, g0:g0 + ATT_DIM] = (y * scale).astype(dst.dtype)


def rope_qk(z3, tables, att_w, ctx_len):
    bsz, t, _ = z3.shape
    rb = _divisor(math.gcd(ctx_len, t - ctx_len), ROW_BLOCK, 8)
    tbl_spec = pl.BlockSpec((rb, ATT_DIM), lambda b, i: (i, 0))
    out_spec = pl.BlockSpec((None, rb, att_w), lambda b, i: (b, i, 0))
    return pl.pallas_call(
        functools.partial(_rope_kernel, q_scale=ATT_DIM ** -0.5 * math.log2(math.e)),
        grid=(bsz, t // rb),
        in_specs=[pl.BlockSpec((None, rb, att_w), lambda b, i: (b, i, 0)),
                  pl.BlockSpec((None, rb, att_w), lambda b, i: (b, i, 1)),
                  tbl_spec, tbl_spec, tbl_spec],
        out_specs=[out_spec, out_spec],
        out_shape=[jax.ShapeDtypeStruct((bsz, t, att_w), BF16)] * 2,
        compiler_params=_params(("parallel", "parallel")),
        name="rope_qk",
    )(z3, z3, *tables)


def _attn_kernel(lam_ref, g_ref, q_ref, k_ref, v_ref, prev_ref, o_ref, acc1_ref, acc2_ref, *, tk, lam_init):
    del prev_ref
    q = q_ref[...]
    q1, q2 = q[:, :ATT_DIM], q[:, ATT_DIM:]
    lv = lam_ref[...]
    lam = (jnp.exp(jnp.sum(lv[0:1] * lv[1:2], axis=-1, keepdims=True))
           - jnp.exp(jnp.sum(lv[2:3] * lv[3:4], axis=-1, keepdims=True)) + lam_init)
    tq = q.shape[0]
    nt = (((1,), (1,)), ((), ()))
    acc1_ref[...] = jnp.zeros(acc1_ref.shape, F32)
    acc2_ref[...] = jnp.zeros(acc2_ref.shape, F32)

    def one_map(qm, km, vc, acc_ref, m, l):
        s = lax.dot_general(qm, km, nt, preferred_element_type=F32)
        m_new = jnp.maximum(m, jnp.max(s, axis=-1, keepdims=True))
        alpha = jnp.exp2(m - m_new)
        p = jnp.exp2(s - m_new)
        l_new = alpha * l + jnp.sum(p, axis=-1, keepdims=True)
        acc_ref[...] = alpha * acc_ref[...] + jnp.dot(p.astype(BF16), vc, preferred_element_type=F32)
        return m_new, l_new

    def body(c, carry):
        m1, l1, m2, l2 = carry
        r0 = pl.multiple_of(c * tk, tk)
        kc = k_ref[pl.ds(r0, tk), :]
        vc = v_ref[pl.ds(r0, tk), :]
        m1, l1 = one_map(q1, kc[:, :ATT_DIM], vc, acc1_ref, m1, l1)
        m2, l2 = one_map(q2, kc[:, ATT_DIM:], vc, acc2_ref, m2, l2)
        return m1, l1, m2, l2

    neg = jnp.full((tq, 1), -jnp.inf, F32)
    zero = jnp.zeros((tq, 1), F32)
    carry = (neg, zero, neg, zero)
    n_chunks = k_ref.shape[0] // tk
    unroll = min(n_chunks, ATT_MAX_UNROLL)
    peeled = n_chunks % unroll
    for c in range(peeled):
        carry = body(c, carry)

    def trip(i, cr):
        for j in range(unroll):
            cr = body(peeled + unroll * i + j, cr)
        return cr

    _, l1, _, l2 = lax.fori_loop(0, n_chunks // unroll, trip, carry)
    o = acc1_ref[...] / l1 - lam * (acc2_ref[...] / l2)
    o = _rms(o, g_ref[...]) * (1.0 - lam_init)
    o_ref[...] = o.astype(o_ref.dtype)


def diff_attention(qr, kr, z3, lam_vec, att_g, *, v_col_blk, q_row0, n_q, n_kv, lam_init, out, heads):
    bsz, t, att_w = qr.shape
    tq = _divisor(math.gcd(q_row0, n_q) if q_row0 else n_q, ROW_BLOCK, 8)
    tk = _divisor(n_kv, 1024, V7X_LANES)
    qoff = q_row0 // tq
    q_spec = pl.BlockSpec((None, tq, ATT_VDIM), lambda b, h, i: (b, i + qoff, h))
    return pl.pallas_call(
        functools.partial(_attn_kernel, tk=tk, lam_init=lam_init),
        grid=(bsz, heads, n_q // tq),
        in_specs=[pl.BlockSpec(lam_vec.shape, lambda b, h, i: (0, 0)),
                  pl.BlockSpec((1, ATT_VDIM), lambda b, h, i: (0, 0)),
                  q_spec,
                  pl.BlockSpec((None, n_kv, ATT_VDIM), lambda b, h, i: (b, 0, h)),
                  pl.BlockSpec((None, n_kv, ATT_VDIM), lambda b, h, i: (b, 0, v_col_blk + h)),
                  pl.BlockSpec(memory_space=pl.ANY)],
        out_specs=q_spec,
        out_shape=jax.ShapeDtypeStruct((bsz, t, att_w), BF16),
        input_output_aliases={5: 0},
        scratch_shapes=[pltpu.VMEM((tq, ATT_VDIM), F32), pltpu.VMEM((tq, ATT_VDIM), F32)],
        compiler_params=_params(("parallel", "parallel", "arbitrary")),
        name="diff_attention",
    )(lam_vec, att_g.reshape(1, ATT_VDIM), qr, kr, z3, out)


def _conv_kernel(cb_ref, cc_ref, ch_ref, w_ref, o_ref, p_ref, *, ctx_len, rc):
    t, tc = cb_ref.shape
    pad = 8
    p_ref[0:pad, :] = jnp.zeros((pad, tc), F32)
    p_ref[pad + t:pad + t + pad, :] = jnp.zeros((pad, tc), F32)
    for r0 in range(0, t, rc):
        p_ref[pad + r0:pad + r0 + rc, :] = cc_ref[r0:r0 + rc, :].astype(F32) * ch_ref[r0:r0 + rc, :].astype(F32)
    w0, w1, w2 = w_ref[0:1, :], w_ref[1:2, :], w_ref[2:3, :]
    for r0 in range(0, t, rc):
        row = r0 + lax.broadcasted_iota(jnp.int32, (rc, 1), 0)
        prev = p_ref[pad + r0 - 1:pad + r0 - 1 + rc, :]
        cur = p_ref[pad + r0:pad + r0 + rc, :]
        nxt = p_ref[pad + r0 + 1:pad + r0 + 1 + rc, :]
        prev = jnp.where(row == ctx_len, 0.0, prev)
        nxt = jnp.where(row == ctx_len - 1, 0.0, nxt)
        y = cb_ref[r0:r0 + rc, :].astype(F32) * (w0 * prev + w1 * cur + w2 * nxt)
        o_ref[r0:r0 + rc, :] = y.astype(o_ref.dtype)


def short_conv(z3, conv_w, *, cb_off, ctx_len):
    bsz, t, _ = z3.shape
    conv_wd = conv_w.shape[-1]
    tc = V7X_LANES
    rc = _divisor(t, 768, 8)
    blk0 = cb_off // tc
    nblk = conv_wd // tc
    spec = lambda k: pl.BlockSpec((None, t, tc), lambda b, j: (b, 0, blk0 + k * nblk + j))
    return pl.pallas_call(
        functools.partial(_conv_kernel, ctx_len=ctx_len, rc=rc),
        grid=(bsz, nblk),
        in_specs=[spec(0), spec(1), spec(2), pl.BlockSpec((3, tc), lambda b, j: (0, j))],
        out_specs=pl.BlockSpec((None, t, tc), lambda b, j: (b, 0, j)),
        out_shape=jax.ShapeDtypeStruct((bsz, t, conv_wd), BF16),
        scratch_shapes=[pltpu.VMEM((t + 16, tc), F32)],
        compiler_params=_params(("parallel", "parallel")),
        name="short_conv",
    )(z3, z3, z3, conv_w)


def ssm_matrices(a_re, a_im, log_dt, b_re, b_im, c_re, c_im, ssm_d, gp):
    tc = SSM_CHUNK
    n_g, n_p = a_re.shape[1], a_re.shape[2]
    n_i = b_re.shape[-1]
    dt = jnp.exp(log_dt.astype(F32))[..., None]
    lr = jnp.minimum(a_re.astype(F32), -1e-4)
    li = a_im.astype(F32)

    def power(n):
        nn = n.astype(F32)[:, None, None, None]
        mag = jnp.exp(nn * (lr * dt))
        return mag * jnp.cos(nn * (li * dt)), mag * jnp.sin(nn * (li * dt))

    ar, ai = power(jnp.ones((1,), F32))
    ar, ai = ar[0], ai[0]
    den = lr * lr + li * li
    cr = ((ar - 1.0) * lr + ai * li) / den
    ci = (ai * lr - (ar - 1.0) * li) / den
    bre, bim = b_re.astype(F32), b_im.astype(F32)
    br = cr[..., None] * bre - ci[..., None] * bim
    bi = cr[..., None] * bim + ci[..., None] * bre
    cre, cim = c_re.astype(F32), c_im.astype(F32)

    pr, pi = power(jnp.arange(tc + 1))
    wr = pr[..., None] * br - pi[..., None] * bi
    wi = pr[..., None] * bi + pi[..., None] * br
    kern = jnp.einsum('dgip,ndgpj->ndgij', cre, wr) - jnp.einsum('dgip,ndgpj->ndgij', cim, wi)

    r_idx = jnp.arange(tc)[:, None]
    s_idx = jnp.arange(tc)[None, :]

    def toeplitz(lag, d):
        blk = kern[jnp.clip(lag, 0, tc), d]
        blk = jnp.where((lag >= 0)[:, :, None, None, None], blk, 0.0)
        return blk.transpose(2, 0, 4, 1, 3).reshape(n_g, tc * n_i, tc * n_i)

    w = jnp.stack([toeplitz(s_idx - r_idx, 0), toeplitz(r_idx - s_idx, 1)])

    def lane_pad_cols(m):
        q = jnp.arange(n_g) % gp
        onehot = jax.nn.one_hot(q, gp, dtype=F32)
        return (m[:, :, None, :] * onehot[:, None, :, None]).reshape(n_g, m.shape[1], gp * n_p)

    def state_in(d, n_of_r):
        sel = n_of_r
        re = wr[sel, d].transpose(1, 0, 3, 2).reshape(n_g, tc * n_i, n_p)
        im = wi[sel, d].transpose(1, 0, 3, 2).reshape(n_g, tc * n_i, n_p)
        return jnp.stack([lane_pad_cols(re), lane_pad_cols(im)])

    we = jnp.stack([state_in(0, tc - 1 - jnp.arange(tc)), state_in(1, jnp.arange(tc))])

    def state_out(d, n_of_s):
        prs, pis = pr[n_of_s, d], pi[n_of_s, d]
        qr = cre[d][None] * prs[:, :, None, :] - cim[d][None] * pis[:, :, None, :]
        qi = cre[d][None] * pis[:, :, None, :] + cim[d][None] * prs[:, :, None, :]
        fre = qr.transpose(1, 0, 2, 3).reshape(n_g, tc * n_i, n_p)
        fim = -qi.transpose(1, 0, 2, 3).reshape(n_g, tc * n_i, n_p)
        return jnp.stack([lane_pad_cols(fre), lane_pad_cols(fim)]).transpose(0, 1, 3, 2)

    wf = jnp.stack([state_out(0, jnp.arange(tc) + 1), state_out(1, tc - jnp.arange(tc))])

    at = jnp.stack([pr[tc].reshape(2, n_g * n_p), pi[tc].reshape(2, n_g * n_p)], axis=1)
    dd = jnp.tile(ssm_d.astype(F32).reshape(n_g, 1, n_i), (1, 1, tc))
    return w.astype(BF16), we.astype(BF16), wf.astype(BF16), at, dd


def _ssm_state_kernel(u_ref, we_ref, sre_ref, sim_ref):
    gp = u_ref.shape[0]
    for d in range(2):
        for part, dst in ((0, sre_ref), (1, sim_ref)):
            acc = jnp.dot(u_ref[0], we_ref[d, part, 0], preferred_element_type=F32)
            for q in range(1, gp):
                acc = acc + jnp.dot(u_ref[q], we_ref[d, part, q], preferred_element_type=F32)
            dst[d] = acc


def _ssm_scan_kernel(sre_ref, sim_ref, at_ref, hre_ref, him_ref, *, bsz, nch, nctx):
    d = pl.program_id(0)
    atr, ati = at_ref[0:1, :], at_ref[1:2, :]
    zero = jnp.zeros(atr.shape, F32)

    def body(k, carry):
        c_rev = jnp.where(k < nctx, nctx - 1 - k, nch - 1 - (k - nctx))
        c = jnp.where(d == 0, k, c_rev)
        new = []
        for b in range(bsz):
            hr, hi = carry[2 * b], carry[2 * b + 1]
            row = pl.ds(b * nch + c, 1)
            hre_ref[row, :] = hr
            him_ref[row, :] = hi
            new.append(atr * hr - ati * hi + sre_ref[row, :])
            new.append(atr * hi + ati * hr + sim_ref[row, :])
        return tuple(new)

    lax.fori_loop(0, nch, body, tuple([zero] * (2 * bsz)))


def _ssm_out_kernel(u_ref, w_ref, wf_ref, hre_ref, him_ref, dd_ref, y_ref):
    gp = u_ref.shape[0]
    for q in range(gp):
        u = u_ref[q]
        y = u.astype(F32) * dd_ref[q]
        for d in range(2):
            y = y + jnp.dot(u, w_ref[d, q], preferred_element_type=F32)
            y = y + jnp.dot(hre_ref[d].astype(BF16), wf_ref[d, 0, q], preferred_element_type=F32)
            y = y + jnp.dot(him_ref[d].astype(BF16), wf_ref[d, 1, q], preferred_element_type=F32)
        y_ref[q] = y.astype(y_ref.dtype)


def chunk_permutation(n_i):
    tc = SSM_CHUNK
    gs = V7X_LANES // n_i
    r, g, j = jnp.meshgrid(jnp.arange(tc), jnp.arange(gs), jnp.arange(n_i), indexing='ij')
    src = (r * V7X_LANES + g * n_i + j).reshape(-1)
    dst = (g * (tc * n_i) + r * n_i + j).reshape(-1)
    return jnp.zeros((tc * V7X_LANES, tc * V7X_LANES), BF16).at[src, dst].set(1.0)


def _ssm_in_kernel(z_ref, perm_ref, ug_ref, x32_ref):
    tc = SSM_CHUNK
    nchb = z_ref.shape[0] // tc
    x32_ref[...] = z_ref[...].astype(F32)
    steps = [x32_ref[pl.ds(r, nchb, stride=tc), :] for r in range(tc)]
    x = jnp.concatenate(steps, axis=1).astype(BF16)
    xp = jnp.dot(x, perm_ref[...], preferred_element_type=F32)
    wg = ug_ref.shape[2]
    for g in range(ug_ref.shape[0]):
        ug_ref[g] = xp[:, g * wg:(g + 1) * wg].astype(ug_ref.dtype)


def ssm_chunk_layout(z, perm, *, col_off, width, n_g):
    m = z.shape[0]
    tc = SSM_CHUNK
    n_i = width // n_g
    gs = V7X_LANES // n_i
    tm = _divisor(m, SSM_LAYOUT_ROWS, 16 * tc)
    return pl.pallas_call(
        _ssm_in_kernel,
        grid=(m // tm, width // V7X_LANES),
        in_specs=[pl.BlockSpec((tm, V7X_LANES), lambda i, s: (i, col_off // V7X_LANES + s)),
                  pl.BlockSpec(perm.shape, lambda i, s: (0, 0))],
        out_specs=pl.BlockSpec((gs, tm // tc, tc * n_i), lambda i, s: (s, i, 0)),
        out_shape=jax.ShapeDtypeStruct((n_g, m // tc, tc * n_i), BF16),
        scratch_shapes=[pltpu.VMEM((tm, V7X_LANES), F32)],
        compiler_params=_params(("parallel", "parallel")),
        name="ssm_chunk_layout",
    )(z, perm)


def ssm_branch(ug, mats, *, bsz, ctx_len):
    w, we, wf, at, dd = mats
    tc = SSM_CHUNK
    n_g, mc, _ = ug.shape
    n_i = ug.shape[2] // tc
    gpp = we.shape[-1]
    n_p = at.shape[-1] // n_g
    gp = gpp // n_p
    nch = mc // bsz

    s_spec = pl.BlockSpec((2, mc, gpp), lambda g: (0, 0, g))
    s_shape = jax.ShapeDtypeStruct((2, mc, n_g * n_p), F32)
    sre, sim = pl.pallas_call(
        _ssm_state_kernel,
        grid=(n_g // gp,),
        in_specs=[pl.BlockSpec((gp, mc, tc * n_i), lambda g: (g, 0, 0)),
                  pl.BlockSpec((2, 2, gp, tc * n_i, gpp), lambda g: (0, 0, g, 0, 0))],
        out_specs=[s_spec, s_spec],
        out_shape=[s_shape, s_shape],
        compiler_params=_params(("parallel",)),
        name="ssm_chunk_state",
    )(ug, we)

    lanes = _divisor(n_g * n_p, 4 * V7X_LANES, V7X_LANES)
    st_spec = pl.BlockSpec((None, mc, lanes), lambda di, j: (di, 0, j))
    hre, him = pl.pallas_call(
        functools.partial(_ssm_scan_kernel, bsz=bsz, nch=nch, nctx=ctx_len // tc),
        grid=(2, n_g * n_p // lanes),
        in_specs=[st_spec, st_spec, pl.BlockSpec((None, 2, lanes), lambda di, j: (di, 0, j))],
        out_specs=[st_spec, st_spec],
        out_shape=[s_shape, s_shape],
        compiler_params=_params(("parallel", "parallel")),
        name="ssm_chunk_scan",
    )(sre, sim, at)

    yg = pl.pallas_call(
        _ssm_out_kernel,
        grid=(n_g // gp,),
        in_specs=[pl.BlockSpec((gp, mc, tc * n_i), lambda g: (g, 0, 0)),
                  pl.BlockSpec((2, gp, tc * n_i, tc * n_i), lambda g: (0, g, 0, 0)),
                  pl.BlockSpec((2, 2, gp, gpp, tc * n_i), lambda g: (0, 0, g, 0, 0)),
                  s_spec, s_spec,
                  pl.BlockSpec((gp, 1, tc * n_i), lambda g: (g, 0, 0))],
        out_specs=pl.BlockSpec((gp, mc, tc * n_i), lambda g: (g, 0, 0)),
        out_shape=jax.ShapeDtypeStruct((n_g, mc, tc * n_i), BF16),
        compiler_params=_params(("parallel",)),
        name="ssm_chunk_out",
    )(ug, w, wf, hre, him, dd)
    return yg


def _ssm_unchunk_kernel(yg_ref, perm_t_ref, y_ref, y32_ref):
    tc = SSM_CHUNK
    gs, nchb, _ = yg_ref.shape
    yp = jnp.concatenate([yg_ref[g] for g in range(gs)], axis=1)
    ys = jnp.dot(yp, perm_t_ref[...], preferred_element_type=F32)
    for r in range(tc):
        y32_ref[pl.ds(r, nchb, stride=tc), :] = ys[:, r * V7X_LANES:(r + 1) * V7X_LANES]
    y_ref[...] = y32_ref[...].astype(y_ref.dtype)


def ssm_unchunk(yg, perm_t):
    tc = SSM_CHUNK
    n_g, mc, wg = yg.shape
    m = mc * tc
    gs = perm_t.shape[0] // wg
    tm = _divisor(m, SSM_LAYOUT_ROWS, 16 * tc)
    return pl.pallas_call(
        _ssm_unchunk_kernel,
        grid=(m // tm, n_g // gs),
        in_specs=[pl.BlockSpec((gs, tm // tc, wg), lambda i, s: (s, i, 0)),
                  pl.BlockSpec(perm_t.shape, lambda i, s: (0, 0))],
        out_specs=pl.BlockSpec((tm, V7X_LANES), lambda i, s: (i, s)),
        out_shape=jax.ShapeDtypeStruct((m, n_g * wg // tc), BF16),
        scratch_shapes=[pltpu.VMEM((tm, V7X_LANES), F32)],
        compiler_params=_params(("parallel", "parallel")),
        name="ssm_unchunk",
    )(yg, perm_t)


def _glu_kernel(y_ref, w_ref, b_ref, o_ref):
    y = y_ref[...].astype(F32)
    g = 0.5 * y * (1.0 + jnp.tanh(math.sqrt(2.0 / math.pi) * (y + 0.044715 * (y * y * y))))
    r = jnp.dot(g.astype(BF16), w_ref[...], preferred_element_type=F32) + b_ref[...]
    o_ref[...] = (g * jax.nn.sigmoid(r)).astype(o_ref.dtype)


def s5_glu(y, w, b, layer):
    m, width = y.shape
    tm = _divisor(m, 768, 16)
    return pl.pallas_call(
        _glu_kernel,
        grid=(m // tm,),
        in_specs=[pl.BlockSpec((tm, width), lambda i: (i, 0)),
                  pl.BlockSpec((None, width, width), lambda i: (layer, 0, 0)),
                  pl.BlockSpec((1, width), lambda i: (0, 0))],
        out_specs=pl.BlockSpec((tm, width), lambda i: (i, 0)),
        out_shape=jax.ShapeDtypeStruct((m, width), BF16),
        compiler_params=_params(("parallel",)),
        name="s5_glu",
    )(y, w, b.reshape(1, width))


def _merge_kernel(att_ref, conv_ref, ssm_ref, g0_ref, g1_ref, g2_ref, pa_ref, pc_ref, ps_ref, o_ref):
    def branch(x_ref, p_ref, g_ref):
        y = jnp.dot(x_ref[...], p_ref[...], preferred_element_type=F32)
        return jax.nn.sigmoid(g_ref[...].astype(F32)) * y

    m = branch(att_ref, pa_ref, g0_ref) + branch(conv_ref, pc_ref, g1_ref) + branch(ssm_ref, ps_ref, g2_ref)
    o_ref[...] = m.astype(o_ref.dtype)


def branch_merge(att, conv, ssm, z, p_att, p_conv, p_ssm, layer, *, g_off):
    m, d = att.shape[0], p_att.shape[2]
    tm = _divisor(m, 768, 16)
    tn = _divisor(math.gcd(d, g_off), 1024, V7X_LANES)
    gblk = g_off // tn
    nj = d // tn
    row = lambda a: pl.BlockSpec((tm, a.shape[1]), lambda i, j: (i, 0))
    gate = lambda k: pl.BlockSpec((tm, tn), lambda i, j: (i, gblk + k * nj + j))
    col = lambda p: pl.BlockSpec((None, p.shape[1], tn), lambda i, j: (layer, 0, j))
    return pl.pallas_call(
        _merge_kernel,
        grid=(m // tm, nj),
        in_specs=[row(att), row(conv), row(ssm), gate(0), gate(1), gate(2), col(p_att), col(p_conv), col(p_ssm)],
        out_specs=pl.BlockSpec((tm, tn), lambda i, j: (i, j)),
        out_shape=jax.ShapeDtypeStruct((m, d), BF16),
        compiler_params=_params(("parallel", "parallel")),
        name="branch_merge",
    )(att, conv, ssm, z, z, z, p_att, p_conv, p_ssm)


def _first_max(x, axis, iota):
    mx = jnp.max(x, axis=axis, keepdims=True)
    n = x.shape[axis]
    idx = jnp.min(jnp.where(x == mx, iota, n), axis=axis, keepdims=True)
    return mx, idx, iota == idx


def _router_kernel(lg_ref, b_ref, idx_ref, wgt_ref):
    n_e, tm = lg_ref.shape
    per = n_e // N_EXPERT_GROUPS
    scores = jax.nn.sigmoid(lg_ref[...])
    sel = scores + b_ref[...]
    neg = -jnp.inf
    grp = sel.reshape(N_EXPERT_GROUPS, per, tm)
    iota_e = lax.broadcasted_iota(jnp.int32, grp.shape, 1)
    m1, _, first = _first_max(grp, 1, iota_e)
    m2 = jnp.max(jnp.where(first, neg, grp), axis=1, keepdims=True)
    gscore = m1 + m2
    iota_g = lax.broadcasted_iota(jnp.int32, gscore.shape, 0)
    gmask = jnp.zeros(gscore.shape, F32)
    for _ in range(TOPK_GROUPS):
        _, _, hit = _first_max(gscore, 0, iota_g)
        gmask = jnp.where(hit, 1.0, gmask)
        gscore = jnp.where(hit, neg, gscore)
    cand = jnp.where(jnp.broadcast_to(gmask, grp.shape) > 0.0, grp, neg).reshape(n_e, tm)
    iota_x = lax.broadcasted_iota(jnp.int32, cand.shape, 0)
    chosen = jnp.zeros(cand.shape, F32)
    for r in range(TOP_K):
        _, idx, hit = _first_max(cand, 0, iota_x)
        chosen = jnp.where(hit, 1.0, chosen)
        cand = jnp.where(hit, neg, cand)
        idx_ref[r:r + 1, :] = idx
        wgt_ref[r:r + 1, :] = jnp.sum(jnp.where(hit, scores, 0.0), axis=0, keepdims=True)
    norm = ROUTED_SCALE / jnp.sum(chosen * scores, axis=0, keepdims=True)
    wgt_ref[...] = wgt_ref[...] * norm


def route(logits_t, router_b):
    n_e, m = logits_t.shape
    tm = _divisor(m, 512, V7X_LANES)
    spec = lambda r: pl.BlockSpec((r, tm), lambda i: (0, i))
    return pl.pallas_call(
        _router_kernel,
        grid=(m // tm,),
        in_specs=[spec(n_e), pl.BlockSpec((n_e, 1), lambda i: (0, 0))],
        out_specs=[spec(TOP_K), spec(TOP_K)],
        out_shape=[jax.ShapeDtypeStruct((TOP_K, m), jnp.int32), jax.ShapeDtypeStruct((TOP_K, m), F32)],
        compiler_params=_params(("parallel",)),
        name="moe_route",
    )(logits_t, router_b.reshape(n_e, 1))


MOE_TILE = 512


def _rank_kernel(eidx_ref, rank_ref, cnt_ref):
    @pl.when(pl.program_id(0) == 0)
    def _():
        cnt_ref[...] = jnp.zeros(cnt_ref.shape, F32)

    n_e = cnt_ref.shape[0]
    tm = eidx_ref.shape[1]
    earlier = (lax.broadcasted_iota(jnp.int32, (tm, tm), 0) < lax.broadcasted_iota(jnp.int32, (tm, tm), 1))
    upper = jnp.where(earlier, 1.0, 0.0).astype(BF16)
    iota_e = lax.broadcasted_iota(jnp.int32, (n_e, tm), 0)
    base = cnt_ref[...]
    for r in range(TOP_K):
        onehot = jnp.where(iota_e == eidx_ref[r:r + 1, :], 1.0, 0.0)
        before = jnp.dot(onehot.astype(BF16), upper, preferred_element_type=F32)
        rank_ref[r:r + 1, :] = jnp.sum(onehot * (base + before), axis=0, keepdims=True).astype(jnp.int32)
        base = base + jnp.sum(onehot, axis=1, keepdims=True)
    cnt_ref[...] = base


def moe_rank(eidx, n_e):
    _, m = eidx.shape
    tm = _divisor(m, 256, V7X_LANES)
    return pl.pallas_call(
        _rank_kernel,
        grid=(m // tm,),
        in_specs=[pl.BlockSpec((TOP_K, tm), lambda i: (0, i))],
        out_specs=[pl.BlockSpec((TOP_K, tm), lambda i: (0, i)), pl.BlockSpec((n_e, 1), lambda i: (0, 0))],
        out_shape=[jax.ShapeDtypeStruct((TOP_K, m), jnp.int32), jax.ShapeDtypeStruct((n_e, 1), F32)],
        compiler_params=_params(("arbitrary",)),
        name="moe_rank",
    )(eidx)


def _foreach(lo, hi, fn):
    def body(j, carry):
        fn(j)
        return carry

    lax.fori_loop(lo, hi, body, 0)


def _dispatch_kernel(pad_ref, pos_ref, x_ref, xs_ref, zero_ref, sem, zsem, *, n_e):
    tm = x_ref.shape[0]
    tr = zero_ref.shape[0]
    n_tiles = xs_ref.shape[0] // tr

    def zero_row_copy(row):
        return pltpu.make_async_copy(zero_ref.at[pl.ds(0, 1)], xs_ref.at[pl.ds(row, 1)], zsem)

    def zero_tile_copy(tile):
        return pltpu.make_async_copy(zero_ref, xs_ref.at[pl.ds(tile * tr, tr)], zsem)

    @pl.when(pl.program_id(0) == 0)
    def _():
        zero_ref[...] = jnp.zeros(zero_ref.shape, zero_ref.dtype)

        first_unused = pad_ref[2 * n_e]

        def start_rows(e):
            _foreach(0, pad_ref[n_e + e], lambda j: zero_row_copy(pad_ref[e] + j).start())

        def wait_rows(e):
            _foreach(0, pad_ref[n_e + e], lambda j: zero_row_copy(0).wait())

        def fill_expert(e):
            start_rows(e)
            wait_rows(e - 1)

        start_rows(0)
        _foreach(1, n_e, fill_expert)
        wait_rows(n_e - 1)
        _foreach(first_unused, n_tiles, lambda t: zero_tile_copy(t).start())
        _foreach(first_unused, n_tiles, lambda t: zero_tile_copy(0).wait())

    def scatter_token(n):
        for r in range(TOP_K):
            pltpu.make_async_copy(x_ref.at[pl.ds(n, 1)], xs_ref.at[pl.ds(pos_ref[r, n], 1)], sem).start()

    _foreach(0, tm, scatter_token)
    for r in range(TOP_K):
        pltpu.make_async_copy(x_ref, xs_ref.at[pl.ds(0, tm)], sem).wait()


def moe_dispatch(hp, pos, pad_info, n_rows, n_e):
    m, w = hp.shape
    tm = _divisor(m, 256, V7X_LANES)
    grid_spec = pltpu.PrefetchScalarGridSpec(
        num_scalar_prefetch=1,
        grid=(m // tm,),
        in_specs=[pl.BlockSpec((TOP_K, tm), lambda i, pad: (0, i), memory_space=pltpu.SMEM),
                  pl.BlockSpec((tm, w), lambda i, pad: (i, 0))],
        out_specs=pl.BlockSpec(memory_space=pl.ANY),
        scratch_shapes=[pltpu.VMEM((MOE_TILE, w), jnp.uint32), pltpu.SemaphoreType.DMA(()),
                        pltpu.SemaphoreType.DMA(())],
    )
    return pl.pallas_call(
        functools.partial(_dispatch_kernel, n_e=n_e),
        grid_spec=grid_spec,
        out_shape=jax.ShapeDtypeStruct((n_rows, w), jnp.uint32),
        compiler_params=_params(("arbitrary",)),
        name="moe_dispatch",
    )(pad_info, pos, hp)


def _moe_ffn_kernel(te_ref, nv_ref, xs_ref, wg_ref, wu_ref, wd_ref, ys_ref, wgb_ref, wub_ref, wdb_ref):
    i = pl.program_id(0)
    nv = nv_ref[i]
    new_expert = jnp.logical_or(i == 0, te_ref[i] != te_ref[jnp.maximum(i - 1, 0)])

    @pl.when(jnp.logical_and(new_expert, nv > 0))
    def _():
        wgb_ref[...] = wg_ref[...].astype(BF16)
        wub_ref[...] = wu_ref[...].astype(BF16)
        wdb_ref[...] = wd_ref[...].astype(BF16)

    slab = xs_ref.shape[0] // 2

    def ffn_rows(r0):
        rows = r0 + lax.broadcasted_iota(jnp.int32, (slab, 1), 0)
        packed = jnp.where(rows < nv, xs_ref[r0:r0 + slab, :], jnp.uint32(0))
        lo, hi = _unpack_halves(packed)
        x = jnp.concatenate([lo.astype(BF16), hi.astype(BF16)], axis=1)
        g = jnp.dot(x, wgb_ref[...], preferred_element_type=F32)
        u = jnp.dot(x, wub_ref[...], preferred_element_type=F32)
        hid = (g * jax.nn.sigmoid(g) * u).astype(BF16)
        y = jnp.dot(hid, wdb_ref[...], preferred_element_type=F32)
        half = y.shape[1] // 2
        ys_ref[r0:r0 + slab, :] = _pack_halves(y[:, :half], y[:, half:])

    @pl.when(nv > slab)
    def _():
        ffn_rows(0)
        ffn_rows(slab)

    @pl.when(jnp.logical_and(nv > 0, nv <= slab))
    def _():
        ffn_rows(0)
        ys_ref[slab:, :] = jnp.zeros((slab, ys_ref.shape[1]), jnp.uint32)

    @pl.when(nv == 0)
    def _():
        ys_ref[...] = jnp.zeros(ys_ref.shape, jnp.uint32)


def moe_ffn(xs, tile_expert, tile_rows, w_gate, w_up, w_down, layer):
    n_rows, w = xs.shape
    _, _, d, ff = w_gate.shape
    tr = MOE_TILE
    grid_spec = pltpu.PrefetchScalarGridSpec(
        num_scalar_prefetch=2,
        grid=(n_rows // tr,),
        in_specs=[pl.BlockSpec((tr, w), lambda i, te, nv: (i, 0)),
                  pl.BlockSpec((None, None, d, ff), lambda i, te, nv: (layer, te[i], 0, 0)),
                  pl.BlockSpec((None, None, d, ff), lambda i, te, nv: (layer, te[i], 0, 0)),
                  pl.BlockSpec((None, None, ff, d), lambda i, te, nv: (layer, te[i], 0, 0))],
        out_specs=pl.BlockSpec((tr, w), lambda i, te, nv: (i, 0)),
        scratch_shapes=[pltpu.VMEM((d, ff), BF16), pltpu.VMEM((d, ff), BF16), pltpu.VMEM((ff, d), BF16)],
    )
    return pl.pallas_call(
        _moe_ffn_kernel,
        grid_spec=grid_spec,
        out_shape=jax.ShapeDtypeStruct((n_rows, w), jnp.uint32),
        compiler_params=_params(("arbitrary",)),
        name="moe_ffn",
    )(tile_expert, tile_rows, xs, w_gate, w_up, w_down)


def _combine_kernel(pos_ref, w_ref, sh_ref, ys_ref, o_ref, buf_ref, sem, *, cw):
    tm, half = buf_ref.shape[1], buf_ref.shape[2]

    def gather_token(n):
        for r in range(TOP_K):
            pltpu.make_async_copy(ys_ref.at[pl.ds(pos_ref[r, n], 1)], buf_ref.at[r, pl.ds(n, 1)], sem).start()

    _foreach(0, tm, gather_token)
    for r in range(TOP_K):
        pltpu.make_async_copy(ys_ref.at[pl.ds(0, tm)], buf_ref.at[r], sem).wait()
    for c0 in range(0, half, cw):
        acc_lo = sh_ref[:, c0:c0 + cw].astype(F32)
        acc_hi = sh_ref[:, half + c0:half + c0 + cw].astype(F32)
        for r in range(TOP_K):
            lo, hi = _unpack_halves(buf_ref[r, :, c0:c0 + cw])
            wr = w_ref[:, r:r + 1]
            acc_lo = acc_lo + wr * lo
            acc_hi = acc_hi + wr * hi
        o_ref[:, c0:c0 + cw] = acc_lo.astype(o_ref.dtype)
        o_ref[:, half + c0:half + c0 + cw] = acc_hi.astype(o_ref.dtype)


def moe_combine(ys, pos, wgt_t, shared):
    m, d = shared.shape
    w = ys.shape[1]
    tm = _divisor(m, 128, V7X_LANES)
    return pl.pallas_call(
        functools.partial(_combine_kernel, cw=_divisor(w, 256, V7X_LANES)),
        grid=(m // tm,),
        in_specs=[pl.BlockSpec((TOP_K, tm), lambda i: (0, i), memory_space=pltpu.SMEM),
                  pl.BlockSpec((tm, TOP_K), lambda i: (i, 0)),
                  pl.BlockSpec((tm, d), lambda i: (i, 0)),
                  pl.BlockSpec(memory_space=pl.ANY)],
        out_specs=pl.BlockSpec((tm, d), lambda i: (i, 0)),
        out_shape=jax.ShapeDtypeStruct((m, d), BF16),
        scratch_shapes=[pltpu.VMEM((TOP_K, tm, w), jnp.uint32), pltpu.SemaphoreType.DMA(())],
        compiler_params=_params(("arbitrary",)),
        name="moe_combine",
    )(pos, wgt_t, shared, ys)


def _ffn_up_kernel(x_ref, wg_ref, wu_ref, o_ref):
    x = x_ref[...]
    g = jnp.dot(x, wg_ref[...], preferred_element_type=F32)
    u = jnp.dot(x, wu_ref[...], preferred_element_type=F32)
    o_ref[...] = (g * jax.nn.sigmoid(g) * u).astype(o_ref.dtype)


def ffn_up(h, w_gate, w_up, layer):
    m, d = h.shape
    ff = w_gate.shape[2]
    tm = _divisor(m, 768, 16)
    wspec = pl.BlockSpec((None, d, ff), lambda i: (layer, 0, 0))
    return pl.pallas_call(
        _ffn_up_kernel,
        grid=(m // tm,),
        in_specs=[pl.BlockSpec((tm, d), lambda i: (i, 0)), wspec, wspec],
        out_specs=pl.BlockSpec((tm, ff), lambda i: (i, 0)),
        out_shape=jax.ShapeDtypeStruct((m, ff), BF16),
        compiler_params=_params(("parallel",)),
        name="ffn_up",
    )(h, w_gate, w_up)


def _pos_kernel(eidx_ref, rank_ref, start_ref, pos_ref):
    iota_e = lax.broadcasted_iota(jnp.int32, (start_ref.shape[0], eidx_ref.shape[1]), 0)
    for r in range(TOP_K):
        first_row = jnp.sum(jnp.where(iota_e == eidx_ref[r:r + 1, :], start_ref[...], 0), axis=0, keepdims=True)
        pos_ref[r:r + 1, :] = rank_ref[r:r + 1, :] + first_row


def moe_pos(eidx, rank, row_start):
    _, m = eidx.shape
    n_e = row_start.shape[0]
    tm = _divisor(m, 512, V7X_LANES)
    spec = pl.BlockSpec((TOP_K, tm), lambda i: (0, i))
    return pl.pallas_call(
        _pos_kernel,
        grid=(m // tm,),
        in_specs=[spec, spec, pl.BlockSpec((n_e, 1), lambda i: (0, 0))],
        out_specs=spec,
        out_shape=jax.ShapeDtypeStruct((TOP_K, m), jnp.int32),
        compiler_params=_params(("parallel",)),
        name="moe_pos",
    )(eidx, rank, row_start.reshape(n_e, 1))


def moe_sparse(h2, h2p, eidx, wgt, w_gate, w_up, w_down, ws_gate, ws_up, ws_down, layer):
    m, d = h2.shape
    n_e = w_gate.shape[1]
    tr = MOE_TILE
    rank, cnt = moe_rank(eidx, n_e)
    counts = cnt[:, 0].astype(jnp.int32)
    tiles_per = (counts + (tr - 1)) // tr
    tile_end = jnp.cumsum(tiles_per)
    tile_start = tile_end - tiles_per
    pos = moe_pos(eidx, rank, tile_start * tr)
    n_tiles = (TOP_K * m) // tr + n_e
    t_idx = jnp.arange(n_tiles, dtype=jnp.int32)
    tile_expert = jnp.minimum(jnp.sum((tile_end[None, :] <= t_idx[:, None]).astype(jnp.int32), axis=1), n_e - 1)
    owner = tile_expert[:, None] == jnp.arange(n_e, dtype=jnp.int32)[None, :]
    left = jnp.sum(jnp.where(owner, counts[None, :] - (t_idx[:, None] - tile_start[None, :]) * tr, 0), axis=1)
    tile_rows = jnp.clip(left, 0, tr).astype(jnp.int32)
    pad_info = jnp.concatenate([tile_start * tr + counts, tiles_per * tr - counts, tile_end[-1:]]).astype(jnp.int32)
    xs = moe_dispatch(h2p, pos, pad_info, n_tiles * tr, n_e)
    ys = moe_ffn(xs, tile_expert, tile_rows, w_gate, w_up, w_down, layer)
    shared = matmul(ffn_up(h2, ws_gate, ws_up, layer), ws_down, layer)
    return moe_combine(ys, pos, wgt.T, shared)


def _router_split(router_w):
    rw_t = router_w.T
    rw_hi = rw_t.astype(BF16)
    rw_lo = (rw_t - rw_hi.astype(F32)).astype(BF16)
    return rw_hi, rw_lo


def kernel(x, c, ctx, c_ctx, ada_w, ada_b, norm_g, w_in, lam_vec, att_g, conv_w,
           ssm_a_re, ssm_a_im, ssm_log_dt, ssm_b_re, ssm_b_im, ssm_c_re, ssm_c_im, ssm_d,
           glu_w, glu_b, p_att, p_conv, p_ssm, w_o, router_w, router_b,
           exp_w_gate, exp_w_up, exp_w_down, sh_w_gate, sh_w_up, sh_w_down):
    bsz, n_lat, d = x.shape
    n_ctx = ctx.shape[1]
    t = n_ctx + n_lat
    m = bsz * t
    depth = ada_w.shape[0]
    att_w = p_att.shape[1]
    heads = att_w // ATT_VDIM
    conv_wd = conv_w.shape[-1]
    ssm_wd = ssm_d.shape[-1]
    n_e = router_w.shape[-1]
    n_p = ssm_a_re.shape[-1]
    assert bsz + 1 <= 8 and n_ctx % SSM_CHUNK == 0 and n_lat % SSM_CHUNK == 0 and n_lat % GRID_W == 0
    assert V7X_LANES % n_p == 0 and n_e % N_EXPERT_GROUPS == 0
    k_off = att_w
    v_off = 2 * att_w
    cb_off = 3 * att_w
    su_off = cb_off + 3 * conv_wd
    g_off = su_off + ssm_wd

    stream = jnp.concatenate([ctx, x], axis=1)
    cond_cols = jnp.zeros((d, 8), F32).at[:, :bsz].set(c.T).at[:, bsz].set(c_ctx)
    tables = rope_tables(n_ctx, n_lat)
    ssm_groups = ssm_a_re.shape[2]
    perm = chunk_permutation(ssm_wd // ssm_groups)
    w_in_b, p_att_b, p_conv_b, p_ssm_b, w_o_b, glu_w_b, ws_gate_b, ws_up_b, ws_down_b = (
        a.astype(BF16) for a in (w_in, p_att, p_conv, p_ssm, w_o, glu_w, sh_w_gate, sh_w_up, sh_w_down))

    pending = None
    for i in range(depth):
        lam_init = 0.8 - 0.6 * math.exp(-0.3 * i)
        mod8 = modulation(cond_cols, ada_w, ada_b, bsz + 1, i).reshape(8, N_MOD, d)
        mod = jnp.stack([jnp.broadcast_to(mod8[bsz], (bsz, N_MOD, d)), mod8[:bsz]], axis=1)
        gains = norm_g[i]

        if pending is None:
            (h,) = resid_adaln(stream, None, mod, gains, ctx_len=n_ctx, pre_idx=0, shift_idx=0, scale_idx=1)
        else:
            branch, p_mod, p_gains = pending
            stream, h = resid_adaln(stream, branch, jnp.concatenate([mod, p_mod], axis=2),
                                    jnp.concatenate([gains, p_gains], axis=0), ctx_len=n_ctx,
                                    post_idx=4 + 3, gate_idx=N_MOD + 5, pre_idx=0, shift_idx=0, scale_idx=1)
        z = matmul(h.reshape(m, d), w_in_b, i)
        z3 = z.reshape(bsz, t, -1)

        qr, kr = rope_qk(z3, tables, att_w, n_ctx)
        att = jnp.zeros((bsz, t, att_w), BF16)
        att = diff_attention(qr, kr, z3, lam_vec[i], att_g[i], v_col_blk=v_off // ATT_VDIM, q_row0=0, n_q=n_ctx,
                             n_kv=n_ctx, lam_init=lam_init, out=att, heads=heads)
        att = diff_attention(qr, kr, z3, lam_vec[i], att_g[i], v_col_blk=v_off // ATT_VDIM, q_row0=n_ctx, n_q=n_lat,
                             n_kv=t, lam_init=lam_init, out=att, heads=heads)

        conv = short_conv(z3, conv_w[i], cb_off=cb_off, ctx_len=n_ctx)

        mats = ssm_matrices(ssm_a_re[i], ssm_a_im[i], ssm_log_dt[i], ssm_b_re[i], ssm_b_im[i],
                            ssm_c_re[i], ssm_c_im[i], ssm_d[i], V7X_LANES // n_p)
        ug = ssm_chunk_layout(z, perm, col_off=su_off, width=ssm_wd, n_g=ssm_groups)
        yg = ssm_branch(ug, mats, bsz=bsz, ctx_len=n_ctx)
        ssm = s5_glu(ssm_unchunk(yg, perm.T), glu_w_b, glu_b[i], i)

        merged = branch_merge(att.reshape(m, att_w), conv.reshape(m, conv_wd), ssm, z,
                              p_att_b, p_conv_b, p_ssm_b, i, g_off=g_off)
        o = matmul(merged, w_o_b, i).reshape(bsz, t, d)

        stream, h2, h2p, logits_t = resid_adaln(stream, o, mod, gains, ctx_len=n_ctx, post_idx=1, gate_idx=2,
                                                pre_idx=2, shift_idx=3, scale_idx=4,
                                                router=_router_split(router_w[i]))
        eidx, wgt = route(logits_t, router_b[i])
        f = moe_sparse(h2.reshape(m, d), h2p.reshape(m, d // 2), eidx, wgt, exp_w_gate, exp_w_up, exp_w_down,
                       ws_gate_b, ws_up_b, ws_down_b, i).reshape(bsz, t, d)
        pending = (f, mod, gains)

    branch, p_mod, p_gains = pending
    (out,) = resid_adaln(stream, branch, p_mod, p_gains, ctx_len=n_ctx, post_idx=3, gate_idx=5,
                         latent_only_out=True)
    return out
```

```python
import functools
import math

import jax
import jax.numpy as jnp
from jax import lax
from jax.experimental import pallas as pl
from jax.experimental.pallas import tpu as pltpu

F32 = jnp.float32
BF16 = jnp.bfloat16

GRID_W = 64
ROPE_BASE = 10000.0
ATT_DIM = 128
ATT_VDIM = 2 * ATT_DIM
N_BRANCH = 3
TOP_K = 8
N_EXPERT_GROUPS = 8
TOPK_GROUPS = 4
ROUTED_SCALE = 2.5
N_MOD = 6
EPS = 1e-6
SSM_CHUNK = 16
ATT_MAX_UNROLL = 16
ATT_KV_CHUNK = 1024
SSM_LAYOUT_ROWS = 8448

V7X_LANES = 128
V7X_VMEM_LIMIT_BYTES = 56 * 1024 * 1024
ROW_BLOCK = 256


def _divisor(n, target, mult):
    best = None
    for d in range(mult, min(n, target) + 1, mult):
        if n % d == 0:
            best = d
    return best if best is not None else n


def _params(sem):
    return pltpu.CompilerParams(dimension_semantics=sem, vmem_limit_bytes=V7X_VMEM_LIMIT_BYTES)


def _mod_kernel(s_ref, w_ref, b_ref, o_ref, *, rows, kc):
    s = s_ref[...]
    s = s * jax.nn.sigmoid(s)
    o_ref[...] = jnp.zeros(o_ref.shape, F32)
    d = w_ref.shape[0]
    for r in range(rows):
        acc = b_ref[...]
        for k0 in range(0, d, kc):
            acc = acc + jnp.sum(w_ref[k0:k0 + kc, :] * s[k0:k0 + kc, r:r + 1], axis=0, keepdims=True)
        o_ref[r:r + 1, :] = acc


def modulation(cond_cols, w, b, rows, layer):
    n_l, d, n = w.shape
    tn = _divisor(n, 512, V7X_LANES)
    kc = _divisor(d, 512, 8)
    return pl.pallas_call(
        functools.partial(_mod_kernel, rows=rows, kc=kc),
        grid=(n // tn,),
        in_specs=[pl.BlockSpec((d, 8), lambda j: (0, 0)),
                  pl.BlockSpec((None, d, tn), lambda j: (layer, 0, j)),
                  pl.BlockSpec((None, 1, tn), lambda j: (layer, 0, j))],
        out_specs=pl.BlockSpec((8, tn), lambda j: (0, j)),
        out_shape=jax.ShapeDtypeStruct((8, n), F32),
        compiler_params=_params(("arbitrary",)),
        name="modulation",
    )(cond_cols, w, b.reshape(n_l, 1, n))


def _rms(x, g):
    return x * lax.rsqrt(jnp.mean(x * x, axis=-1, keepdims=True) + EPS) * g


def _pack_halves(a, b):
    lo = lax.bitcast_convert_type(a.astype(BF16).astype(F32), jnp.uint32) >> 16
    hi = lax.bitcast_convert_type(b.astype(BF16).astype(F32), jnp.uint32) & jnp.uint32(0xFFFF0000)
    return lo | hi


def _unpack_halves(p):
    lo = lax.bitcast_convert_type(p << 16, F32)
    hi = lax.bitcast_convert_type(p & jnp.uint32(0xFFFF0000), F32)
    return lo, hi


def _resid_adaln_kernel(*refs, has_branch, has_h, has_router, post_idx, gate_idx, pre_idx, shift_idx, scale_idx):
    it = iter(refs)
    x_ref = next(it)
    o_ref = next(it) if has_branch else None
    mod_ref = next(it)
    g_ref = next(it)
    rwh_ref = next(it) if has_router else None
    rwl_ref = next(it) if has_router else None
    xout_ref = next(it) if has_branch else None
    h_ref = next(it) if has_h else None
    hp_ref = next(it) if has_router else None
    lg_ref = next(it) if has_router else None

    x = x_ref[...]
    if has_branch:
        o = o_ref[...].astype(F32)
        x = x + mod_ref[gate_idx:gate_idx + 1, :] * _rms(o, g_ref[post_idx:post_idx + 1, :])
        xout_ref[...] = x
    if has_h:
        h = _rms(x, g_ref[pre_idx:pre_idx + 1, :])
        h = h * (1.0 + mod_ref[scale_idx:scale_idx + 1, :]) + mod_ref[shift_idx:shift_idx + 1, :]
        h_ref[...] = h.astype(BF16)
        if has_router:
            h_hi = h.astype(BF16)
            h_lo = (h - h_hi.astype(F32)).astype(BF16)
            nt = (((1,), (1,)), ((), ()))
            lg = lax.dot_general(rwh_ref[...], h_hi, nt, preferred_element_type=F32)
            lg = lg + lax.dot_general(rwh_ref[...], h_lo, nt, preferred_element_type=F32)
            lg = lg + lax.dot_general(rwl_ref[...], h_hi, nt, preferred_element_type=F32)
            lg_ref[...] = lg
            half = h.shape[1] // 2
            hp_ref[...] = _pack_halves(h[:, :half], h[:, half:])


def resid_adaln(x, branch, mod, gains, *, ctx_len, post_idx=0, gate_idx=0, pre_idx=None,
                shift_idx=0, scale_idx=0, router=None, latent_only_out=False):
    bsz, t, d = x.shape
    rb = _divisor(math.gcd(ctx_len, t - ctx_len), ROW_BLOCK, 8)
    nctx = ctx_len // rb
    has_branch = branch is not None
    has_h = pre_idx is not None
    has_router = router is not None
    off = nctx if latent_only_out else 0
    nblk = t // rb - off

    def row_map(b, i):
        return (b, i + off, 0)

    in_specs = [pl.BlockSpec((None, rb, d), row_map)]
    args = [x]
    if has_branch:
        in_specs.append(pl.BlockSpec((None, rb, d), row_map))
        args.append(branch)
    in_specs.append(pl.BlockSpec((None, None, mod.shape[2], d),
                                 lambda b, i: (b, jnp.where(i + off >= nctx, 1, 0), 0, 0)))
    args.append(mod)
    in_specs.append(pl.BlockSpec(gains.shape, lambda b, i: (0, 0)))
    args.append(gains)
    if has_router:
        rw_hi, rw_lo = router
        in_specs += [pl.BlockSpec(rw_hi.shape, lambda b, i: (0, 0)), pl.BlockSpec(rw_lo.shape, lambda b, i: (0, 0))]
        args += [rw_hi, rw_lo]
    out_specs, out_shape = [], []
    if has_branch:
        out_specs.append(pl.BlockSpec((None, rb, d), lambda b, i: (b, i, 0)))
        out_shape.append(jax.ShapeDtypeStruct((bsz, nblk * rb, d), F32))
    if has_h:
        out_specs.append(pl.BlockSpec((None, rb, d), lambda b, i: (b, i, 0)))
        out_shape.append(jax.ShapeDtypeStruct((bsz, t, d), BF16))
    if has_router:
        n_e = router[0].shape[0]
        out_specs.append(pl.BlockSpec((None, rb, d // 2), lambda b, i: (b, i, 0)))
        out_shape.append(jax.ShapeDtypeStruct((bsz, t, d // 2), jnp.uint32))
        out_specs.append(pl.BlockSpec((n_e, rb), lambda b, i: (0, b * nblk + i)))
        out_shape.append(jax.ShapeDtypeStruct((n_e, bsz * t), F32))
    return pl.pallas_call(
        functools.partial(_resid_adaln_kernel, has_branch=has_branch, has_h=has_h, has_router=has_router,
                          post_idx=post_idx, gate_idx=gate_idx, pre_idx=pre_idx, shift_idx=shift_idx,
                          scale_idx=scale_idx),
        grid=(bsz, nblk),
        in_specs=in_specs,
        out_specs=out_specs,
        out_shape=out_shape,
        compiler_params=_params(("parallel", "parallel")),
        name="resid_adaln",
    )(*args)


def _mm_kernel(a_ref, b_ref, o_ref):
    o_ref[...] = jnp.dot(a_ref[...], b_ref[...], preferred_element_type=F32).astype(o_ref.dtype)


def matmul(a, b, layer, out_dtype=BF16, tm_target=768, tn_target=1024):
    m, k = a.shape
    _, _, n = b.shape
    tm = _divisor(m, tm_target, 16)
    tn = _divisor(n, tn_target, V7X_LANES)
    return pl.pallas_call(
        _mm_kernel,
        grid=(m // tm, n // tn),
        in_specs=[pl.BlockSpec((tm, k), lambda i, j: (i, 0)),
                  pl.BlockSpec((None, k, tn), lambda i, j: (layer, 0, j))],
        out_specs=pl.BlockSpec((tm, tn), lambda i, j: (i, j)),
        out_shape=jax.ShapeDtypeStruct((m, n), out_dtype),
        compiler_params=_params(("parallel", "parallel")),
        name="matmul",
    )(a, b)


def rope_tables(ctx_len, n_lat):
    nf = ATT_DIM // 4
    rows = n_lat // GRID_W
    row = jnp.repeat(jnp.arange(rows), GRID_W).astype(F32)
    col = jnp.tile(jnp.arange(GRID_W), rows).astype(F32)
    inv = ROPE_BASE ** (-jnp.arange(nf, dtype=F32) / nf)
    ang_r = row[:, None] * inv
    ang_c = col[:, None] * inv
    zero = jnp.zeros_like(ang_r)
    cos = jnp.concatenate([jnp.cos(ang_r), jnp.cos(ang_r), jnp.cos(ang_c), jnp.cos(ang_c)], axis=-1)
    sa = jnp.concatenate([-jnp.sin(ang_r), zero, -jnp.sin(ang_c), zero], axis=-1)
    sb = jnp.concatenate([zero, jnp.sin(ang_r), zero, jnp.sin(ang_c)], axis=-1)
    pad = lambda tbl, v: jnp.concatenate([jnp.full((ctx_len, ATT_DIM), v, F32), tbl], axis=0)
    return pad(cos, 1.0), pad(sa, 0.0), pad(sb, 0.0)


def _rope_kernel(q_ref, k_ref, cos_ref, sa_ref, sb_ref, qo_ref, ko_ref, *, q_scale):
    cos, sa, sb = cos_ref[...], sa_ref[...], sb_ref[...]
    quarter = ATT_DIM // 4
    for src, dst, scale in ((q_ref, qo_ref, q_scale), (k_ref, ko_ref, 1.0)):
        for g0 in range(0, src.shape[-1], ATT_DIM):
            x = src[:, g0:g0 + ATT_DIM].astype(F32)
            y = x * cos + pltpu.roll(x, ATT_DIM - quarter, 1) * sa + pltpu.roll(x, quarter, 1) * sb
            dst[:, g0:g0 + ATT_DIM] = (y * scale).astype(dst.dtype)


def rope_qk(z3, tables, att_w, ctx_len):
    bsz, t, _ = z3.shape
    rb = _divisor(math.gcd(ctx_len, t - ctx_len), ROW_BLOCK, 8)
    tbl_spec = pl.BlockSpec((rb, ATT_DIM), lambda b, i: (i, 0))
    out_spec = pl.BlockSpec((None, rb, att_w), lambda b, i: (b, i, 0))
    return pl.pallas_call(
        functools.partial(_rope_kernel, q_scale=ATT_DIM ** -0.5 * math.log2(math.e)),
        grid=(bsz, t // rb),
        in_specs=[pl.BlockSpec((None, rb, att_w), lambda b, i: (b, i, 0)),
                  pl.BlockSpec((None, rb, att_w), lambda b, i: (b, i, 1)),
                  tbl_spec, tbl_spec, tbl_spec],
        out_specs=[out_spec, out_spec],
        out_shape=[jax.ShapeDtypeStruct((bsz, t, att_w), BF16)] * 2,
        compiler_params=_params(("parallel", "parallel")),
        name="rope_qk",
    )(z3, z3, *tables)


def _attn_kernel(lam_ref, g_ref, q_ref, k_ref, v_ref, prev_ref, o_ref, acc1_ref, acc2_ref, *, tk, lam_init):
    del prev_ref
    q = q_ref[...]
    q1, q2 = q[:, :ATT_DIM], q[:, ATT_DIM:]
    lv = lam_ref[...]
    lam = (jnp.exp(jnp.sum(lv[0:1] * lv[1:2], axis=-1, keepdims=True))
           - jnp.exp(jnp.sum(lv[2:3] * lv[3:4], axis=-1, keepdims=True)) + lam_init)
    tq = q.shape[0]
    nt = (((1,), (1,)), ((), ()))
    acc1_ref[...] = jnp.zeros(acc1_ref.shape, F32)
    acc2_ref[...] = jnp.zeros(acc2_ref.shape, F32)

    def one_map(qm, km, vc, acc_ref, m, l):
        s = lax.dot_general(qm, km, nt, preferred_element_type=F32)
        m_new = jnp.maximum(m, jnp.max(s, axis=-1, keepdims=True))
        alpha = jnp.exp2(m - m_new)
        p = jnp.exp2(s - m_new)
        l_new = alpha * l + jnp.sum(p, axis=-1, keepdims=True)
        acc_ref[...] = alpha * acc_ref[...] + jnp.dot(p.astype(BF16), vc, preferred_element_type=F32)
        return m_new, l_new

    def body(c, carry):
        m1, l1, m2, l2 = carry
        r0 = pl.multiple_of(c * tk, tk)
        kc = k_ref[pl.ds(r0, tk), :]
        vc = v_ref[pl.ds(r0, tk), :]
        m1, l1 = one_map(q1, kc[:, :ATT_DIM], vc, acc1_ref, m1, l1)
        m2, l2 = one_map(q2, kc[:, ATT_DIM:], vc, acc2_ref, m2, l2)
        return m1, l1, m2, l2

    neg = jnp.full((tq, 1), -jnp.inf, F32)
    zero = jnp.zeros((tq, 1), F32)
    carry = (neg, zero, neg, zero)
    n_chunks = k_ref.shape[0] // tk
    unroll = min(n_chunks, ATT_MAX_UNROLL)
    peeled = n_chunks % unroll
    for c in range(peeled):
        carry = body(c, carry)

    def trip(i, cr):
        for j in range(unroll):
            cr = body(peeled + unroll * i + j, cr)
        return cr

    _, l1, _, l2 = lax.fori_loop(0, n_chunks // unroll, trip, carry)
    o = acc1_ref[...] / l1 - lam * (acc2_ref[...] / l2)
    o = _rms(o, g_ref[...]) * (1.0 - lam_init)
    o_ref[...] = o.astype(o_ref.dtype)


def diff_attention(qr, kr, z3, lam_vec, att_g, *, v_col_blk, q_row0, n_q, n_kv, lam_init, out, heads):
    bsz, t, att_w = qr.shape
    tq = _divisor(math.gcd(q_row0, n_q) if q_row0 else n_q, ROW_BLOCK, 8)
    tk = _divisor(n_kv, ATT_KV_CHUNK, V7X_LANES)
    qoff = q_row0 // tq
    q_spec = pl.BlockSpec((None, tq, ATT_VDIM), lambda b, h, i: (b, i + qoff, h))
    return pl.pallas_call(
        functools.partial(_attn_kernel, tk=tk, lam_init=lam_init),
        grid=(bsz, heads, n_q // tq),
        in_specs=[pl.BlockSpec(lam_vec.shape, lambda b, h, i: (0, 0)),
                  pl.BlockSpec((1, ATT_VDIM), lambda b, h, i: (0, 0)),
                  q_spec,
                  pl.BlockSpec((None, n_kv, ATT_VDIM), lambda b, h, i: (b, 0, h)),
                  pl.BlockSpec((None, n_kv, ATT_VDIM), lambda b, h, i: (b, 0, v_col_blk + h)),
                  pl.BlockSpec(memory_space=pl.ANY)],
        out_specs=q_spec,
        out_shape=jax.ShapeDtypeStruct((bsz, t, att_w), BF16),
        input_output_aliases={5: 0},
        scratch_shapes=[pltpu.VMEM((tq, ATT_VDIM), F32), pltpu.VMEM((tq, ATT_VDIM), F32)],
        compiler_params=_params(("parallel", "parallel", "arbitrary")),
        name="diff_attention",
    )(lam_vec, att_g.reshape(1, ATT_VDIM), qr, kr, z3, out)


def _conv_kernel(cb_ref, cc_ref, ch_ref, w_ref, o_ref, p_ref, *, ctx_len, rc):
    t, tc = cb_ref.shape
    pad = 8
    p_ref[0:pad, :] = jnp.zeros((pad, tc), F32)
    p_ref[pad + t:pad + t + pad, :] = jnp.zeros((pad, tc), F32)
    for r0 in range(0, t, rc):
        p_ref[pad + r0:pad + r0 + rc, :] = cc_ref[r0:r0 + rc, :].astype(F32) * ch_ref[r0:r0 + rc, :].astype(F32)
    w0, w1, w2 = w_ref[0:1, :], w_ref[1:2, :], w_ref[2:3, :]
    for r0 in range(0, t, rc):
        row = r0 + lax.broadcasted_iota(jnp.int32, (rc, 1), 0)
        prev = p_ref[pad + r0 - 1:pad + r0 - 1 + rc, :]
        cur = p_ref[pad + r0:pad + r0 + rc, :]
        nxt = p_ref[pad + r0 + 1:pad + r0 + 1 + rc, :]
        prev = jnp.where(row == ctx_len, 0.0, prev)
        nxt = jnp.where(row == ctx_len - 1, 0.0, nxt)
        y = cb_ref[r0:r0 + rc, :].astype(F32) * (w0 * prev + w1 * cur + w2 * nxt)
        o_ref[r0:r0 + rc, :] = y.astype(o_ref.dtype)


def short_conv(z3, conv_w, *, cb_off, ctx_len):
    bsz, t, _ = z3.shape
    conv_wd = conv_w.shape[-1]
    tc = V7X_LANES
    rc = _divisor(t, 768, 8)
    blk0 = cb_off // tc
    nblk = conv_wd // tc
    spec = lambda k: pl.BlockSpec((None, t, tc), lambda b, j: (b, 0, blk0 + k * nblk + j))
    return pl.pallas_call(
        functools.partial(_conv_kernel, ctx_len=ctx_len, rc=rc),
        grid=(bsz, nblk),
        in_specs=[spec(0), spec(1), spec(2), pl.BlockSpec((3, tc), lambda b, j: (0, j))],
        out_specs=pl.BlockSpec((None, t, tc), lambda b, j: (b, 0, j)),
        out_shape=jax.ShapeDtypeStruct((bsz, t, conv_wd), BF16),
        scratch_shapes=[pltpu.VMEM((t + 16, tc), F32)],
        compiler_params=_params(("parallel", "parallel")),
        name="short_conv",
    )(z3, z3, z3, conv_w)


def ssm_matrices(a_re, a_im, log_dt, b_re, b_im, c_re, c_im, ssm_d, gp):
    tc = SSM_CHUNK
    n_g, n_p = a_re.shape[1], a_re.shape[2]
    n_i = b_re.shape[-1]
    dt = jnp.exp(log_dt.astype(F32))[..., None]
    lr = jnp.minimum(a_re.astype(F32), -1e-4)
    li = a_im.astype(F32)

    def power(n):
        nn = n.astype(F32)[:, None, None, None]
        mag = jnp.exp(nn * (lr * dt))
        return mag * jnp.cos(nn * (li * dt)), mag * jnp.sin(nn * (li * dt))

    ar, ai = power(jnp.ones((1,), F32))
    ar, ai = ar[0], ai[0]
    den = lr * lr + li * li
    cr = ((ar - 1.0) * lr + ai * li) / den
    ci = (ai * lr - (ar - 1.0) * li) / den
    bre, bim = b_re.astype(F32), b_im.astype(F32)
    br = cr[..., None] * bre - ci[..., None] * bim
    bi = cr[..., None] * bim + ci[..., None] * bre
    cre, cim = c_re.astype(F32), c_im.astype(F32)

    pr, pi = power(jnp.arange(tc + 1))
    wr = pr[..., None] * br - pi[..., None] * bi
    wi = pr[..., None] * bi + pi[..., None] * br
    kern = jnp.einsum('dgip,ndgpj->ndgij', cre, wr) - jnp.einsum('dgip,ndgpj->ndgij', cim, wi)

    r_idx = jnp.arange(tc)[:, None]
    s_idx = jnp.arange(tc)[None, :]

    def toeplitz(lag, d):
        blk = kern[jnp.clip(lag, 0, tc), d]
        blk = jnp.where((lag >= 0)[:, :, None, None, None], blk, 0.0)
        return blk.transpose(2, 0, 4, 1, 3).reshape(n_g, tc * n_i, tc * n_i)

    w = jnp.stack([toeplitz(s_idx - r_idx, 0), toeplitz(r_idx - s_idx, 1)])

    def lane_pad_cols(m):
        q = jnp.arange(n_g) % gp
        onehot = jax.nn.one_hot(q, gp, dtype=F32)
        return (m[:, :, None, :] * onehot[:, None, :, None]).reshape(n_g, m.shape[1], gp * n_p)

    def state_in(d, n_of_r):
        sel = n_of_r
        re = wr[sel, d].transpose(1, 0, 3, 2).reshape(n_g, tc * n_i, n_p)
        im = wi[sel, d].transpose(1, 0, 3, 2).reshape(n_g, tc * n_i, n_p)
        return jnp.stack([lane_pad_cols(re), lane_pad_cols(im)])

    we = jnp.stack([state_in(0, tc - 1 - jnp.arange(tc)), state_in(1, jnp.arange(tc))])

    def state_out(d, n_of_s):
        prs, pis = pr[n_of_s, d], pi[n_of_s, d]
        qr = cre[d][None] * prs[:, :, None, :] - cim[d][None] * pis[:, :, None, :]
        qi = cre[d][None] * pis[:, :, None, :] + cim[d][None] * prs[:, :, None, :]
        fre = qr.transpose(1, 0, 2, 3).reshape(n_g, tc * n_i, n_p)
        fim = -qi.transpose(1, 0, 2, 3).reshape(n_g, tc * n_i, n_p)
        return jnp.stack([lane_pad_cols(fre), lane_pad_cols(fim)]).transpose(0, 1, 3, 2)

    wf = jnp.stack([state_out(0, jnp.arange(tc) + 1), state_out(1, tc - jnp.arange(tc))])

    at = jnp.stack([pr[tc].reshape(2, n_g * n_p), pi[tc].reshape(2, n_g * n_p)], axis=1)
    dd = jnp.tile(ssm_d.astype(F32).reshape(n_g, 1, n_i), (1, 1, tc))
    return w.astype(BF16), we.astype(BF16), wf.astype(BF16), at, dd


def _ssm_state_kernel(u_ref, we_ref, sre_ref, sim_ref):
    gp = u_ref.shape[0]
    for d in range(2):
        for part, dst in ((0, sre_ref), (1, sim_ref)):
            acc = jnp.dot(u_ref[0], we_ref[d, part, 0], preferred_element_type=F32)
            for q in range(1, gp):
                acc = acc + jnp.dot(u_ref[q], we_ref[d, part, q], preferred_element_type=F32)
            dst[d] = acc


def _ssm_scan_kernel(sre_ref, sim_ref, at_ref, hre_ref, him_ref, *, bsz, nch, nctx):
    d = pl.program_id(0)
    atr, ati = at_ref[0:1, :], at_ref[1:2, :]
    zero = jnp.zeros(atr.shape, F32)

    def body(k, carry):
        c_rev = jnp.where(k < nctx, nctx - 1 - k, nch - 1 - (k - nctx))
        c = jnp.where(d == 0, k, c_rev)
        new = []
        for b in range(bsz):
            hr, hi = carry[2 * b], carry[2 * b + 1]
            row = pl.ds(b * nch + c, 1)
            hre_ref[row, :] = hr
            him_ref[row, :] = hi
            new.append(atr * hr - ati * hi + sre_ref[row, :])
            new.append(atr * hi + ati * hr + sim_ref[row, :])
        return tuple(new)

    lax.fori_loop(0, nch, body, tuple([zero] * (2 * bsz)))


def _ssm_out_kernel(u_ref, w_ref, wf_ref, hre_ref, him_ref, dd_ref, y_ref):
    gp = u_ref.shape[0]
    for q in range(gp):
        u = u_ref[q]
        y = u.astype(F32) * dd_ref[q]
        for d in range(2):
            y = y + jnp.dot(u, w_ref[d, q], preferred_element_type=F32)
            y = y + jnp.dot(hre_ref[d].astype(BF16), wf_ref[d, 0, q], preferred_element_type=F32)
            y = y + jnp.dot(him_ref[d].astype(BF16), wf_ref[d, 1, q], preferred_element_type=F32)
        y_ref[q] = y.astype(y_ref.dtype)


def chunk_permutation(n_i):
    tc = SSM_CHUNK
    gs = V7X_LANES // n_i
    src = jnp.arange(tc * V7X_LANES)
    r, g, j = src // V7X_LANES, (src % V7X_LANES) // n_i, src % n_i
    dst = g * (tc * n_i) + r * n_i + j
    return (dst[:, None] == jnp.arange(tc * V7X_LANES)[None, :]).astype(BF16)


def _ssm_in_kernel(z_ref, perm_ref, ug_ref, x32_ref):
    tc = SSM_CHUNK
    nchb = z_ref.shape[0] // tc
    x32_ref[...] = z_ref[...].astype(F32)
    steps = [x32_ref[pl.ds(r, nchb, stride=tc), :] for r in range(tc)]
    x = jnp.concatenate(steps, axis=1).astype(BF16)
    xp = jnp.dot(x, perm_ref[...], preferred_element_type=F32)
    wg = ug_ref.shape[2]
    for g in range(ug_ref.shape[0]):
        ug_ref[g] = xp[:, g * wg:(g + 1) * wg].astype(ug_ref.dtype)


def ssm_chunk_layout(z, perm, *, col_off, width, n_g):
    m = z.shape[0]
    tc = SSM_CHUNK
    n_i = width // n_g
    gs = V7X_LANES // n_i
    tm = _divisor(m, SSM_LAYOUT_ROWS, 16 * tc)
    return pl.pallas_call(
        _ssm_in_kernel,
        grid=(m // tm, width // V7X_LANES),
        in_specs=[pl.BlockSpec((tm, V7X_LANES), lambda i, s: (i, col_off // V7X_LANES + s)),
                  pl.BlockSpec(perm.shape, lambda i, s: (0, 0))],
        out_specs=pl.BlockSpec((gs, tm // tc, tc * n_i), lambda i, s: (s, i, 0)),
        out_shape=jax.ShapeDtypeStruct((n_g, m // tc, tc * n_i), BF16),
        scratch_shapes=[pltpu.VMEM((tm, V7X_LANES), F32)],
        compiler_params=_params(("parallel", "parallel")),
        name="ssm_chunk_layout",
    )(z, perm)


def ssm_branch(ug, mats, *, bsz, ctx_len):
    w, we, wf, at, dd = mats
    tc = SSM_CHUNK
    n_g, mc, _ = ug.shape
    n_i = ug.shape[2] // tc
    gpp = we.shape[-1]
    n_p = at.shape[-1] // n_g
    gp = gpp // n_p
    nch = mc // bsz

    s_spec = pl.BlockSpec((2, mc, gpp), lambda g: (0, 0, g))
    s_shape = jax.ShapeDtypeStruct((2, mc, n_g * n_p), F32)
    sre, sim = pl.pallas_call(
        _ssm_state_kernel,
        grid=(n_g // gp,),
        in_specs=[pl.BlockSpec((gp, mc, tc * n_i), lambda g: (g, 0, 0)),
                  pl.BlockSpec((2, 2, gp, tc * n_i, gpp), lambda g: (0, 0, g, 0, 0))],
        out_specs=[s_spec, s_spec],
        out_shape=[s_shape, s_shape],
        compiler_params=_params(("parallel",)),
        name="ssm_chunk_state",
    )(ug, we)

    lanes = _divisor(n_g * n_p, 4 * V7X_LANES, V7X_LANES)
    st_spec = pl.BlockSpec((None, mc, lanes), lambda di, j: (di, 0, j))
    hre, him = pl.pallas_call(
        functools.partial(_ssm_scan_kernel, bsz=bsz, nch=nch, nctx=ctx_len // tc),
        grid=(2, n_g * n_p // lanes),
        in_specs=[st_spec, st_spec, pl.BlockSpec((None, 2, lanes), lambda di, j: (di, 0, j))],
        out_specs=[st_spec, st_spec],
        out_shape=[s_shape, s_shape],
        compiler_params=_params(("parallel", "parallel")),
        name="ssm_chunk_scan",
    )(sre, sim, at)

    yg = pl.pallas_call(
        _ssm_out_kernel,
        grid=(n_g // gp,),
        in_specs=[pl.BlockSpec((gp, mc, tc * n_i), lambda g: (g, 0, 0)),
                  pl.BlockSpec((2, gp, tc * n_i, tc * n_i), lambda g: (0, g, 0, 0)),
                  pl.BlockSpec((2, 2, gp, gpp, tc * n_i), lambda g: (0, 0, g, 0, 0)),
                  s_spec, s_spec,
                  pl.BlockSpec((gp, 1, tc * n_i), lambda g: (g, 0, 0))],
        out_specs=pl.BlockSpec((gp, mc, tc * n_i), lambda g: (g, 0, 0)),
        out_shape=jax.ShapeDtypeStruct((n_g, mc, tc * n_i), BF16),
        compiler_params=_params(("parallel",)),
        name="ssm_chunk_out",
    )(ug, w, wf, hre, him, dd)
    return yg


def _ssm_unchunk_kernel(yg_ref, perm_t_ref, y_ref, y32_ref):
    tc = SSM_CHUNK
    gs, nchb, _ = yg_ref.shape
    yp = jnp.concatenate([yg_ref[g] for g in range(gs)], axis=1)
    ys = jnp.dot(yp, perm_t_ref[...], preferred_element_type=F32)
    for r in range(tc):
        y32_ref[pl.ds(r, nchb, stride=tc), :] = ys[:, r * V7X_LANES:(r + 1) * V7X_LANES]
    y_ref[...] = y32_ref[...].astype(y_ref.dtype)


def ssm_unchunk(yg, perm_t):
    tc = SSM_CHUNK
    n_g, mc, wg = yg.shape
    m = mc * tc
    gs = perm_t.shape[0] // wg
    tm = _divisor(m, SSM_LAYOUT_ROWS, 16 * tc)
    return pl.pallas_call(
        _ssm_unchunk_kernel,
        grid=(m // tm, n_g // gs),
        in_specs=[pl.BlockSpec((gs, tm // tc, wg), lambda i, s: (s, i, 0)),
                  pl.BlockSpec(perm_t.shape, lambda i, s: (0, 0))],
        out_specs=pl.BlockSpec((tm, V7X_LANES), lambda i, s: (i, s)),
        out_shape=jax.ShapeDtypeStruct((m, n_g * wg // tc), BF16),
        scratch_shapes=[pltpu.VMEM((tm, V7X_LANES), F32)],
        compiler_params=_params(("parallel", "parallel")),
        name="ssm_unchunk",
    )(yg, perm_t)


def _glu_kernel(y_ref, w_ref, b_ref, o_ref):
    y = y_ref[...].astype(F32)
    g = 0.5 * y * (1.0 + jnp.tanh(math.sqrt(2.0 / math.pi) * (y + 0.044715 * (y * y * y))))
    r = jnp.dot(g.astype(BF16), w_ref[...], preferred_element_type=F32) + b_ref[...]
    o_ref[...] = (g * jax.nn.sigmoid(r)).astype(o_ref.dtype)


def s5_glu(y, w, b, layer):
    m, width = y.shape
    tm = _divisor(m, 768, 16)
    return pl.pallas_call(
        _glu_kernel,
        grid=(m // tm,),
        in_specs=[pl.BlockSpec((tm, width), lambda i: (i, 0)),
                  pl.BlockSpec((None, width, width), lambda i: (layer, 0, 0)),
                  pl.BlockSpec((1, width), lambda i: (0, 0))],
        out_specs=pl.BlockSpec((tm, width), lambda i: (i, 0)),
        out_shape=jax.ShapeDtypeStruct((m, width), BF16),
        compiler_params=_params(("parallel",)),
        name="s5_glu",
    )(y, w, b.reshape(1, width))


def _merge_kernel(att_ref, conv_ref, ssm_ref, g0_ref, g1_ref, g2_ref, pa_ref, pc_ref, ps_ref, o_ref):
    def branch(x_ref, p_ref, g_ref):
        y = jnp.dot(x_ref[...], p_ref[...], preferred_element_type=F32)
        return jax.nn.sigmoid(g_ref[...].astype(F32)) * y

    m = branch(att_ref, pa_ref, g0_ref) + branch(conv_ref, pc_ref, g1_ref) + branch(ssm_ref, ps_ref, g2_ref)
    o_ref[...] = m.astype(o_ref.dtype)


def branch_merge(att, conv, ssm, z, p_att, p_conv, p_ssm, layer, *, g_off):
    m, d = att.shape[0], p_att.shape[2]
    tm = _divisor(m, 768, 16)
    tn = _divisor(math.gcd(d, g_off), 1024, V7X_LANES)
    gblk = g_off // tn
    nj = d // tn
    row = lambda a: pl.BlockSpec((tm, a.shape[1]), lambda i, j: (i, 0))
    gate = lambda k: pl.BlockSpec((tm, tn), lambda i, j: (i, gblk + k * nj + j))
    col = lambda p: pl.BlockSpec((None, p.shape[1], tn), lambda i, j: (layer, 0, j))
    return pl.pallas_call(
        _merge_kernel,
        grid=(m // tm, nj),
        in_specs=[row(att), row(conv), row(ssm), gate(0), gate(1), gate(2), col(p_att), col(p_conv), col(p_ssm)],
        out_specs=pl.BlockSpec((tm, tn), lambda i, j: (i, j)),
        out_shape=jax.ShapeDtypeStruct((m, d), BF16),
        compiler_params=_params(("parallel", "parallel")),
        name="branch_merge",
    )(att, conv, ssm, z, z, z, p_att, p_conv, p_ssm)


def _first_max(x, axis, iota):
    mx = jnp.max(x, axis=axis, keepdims=True)
    n = x.shape[axis]
    idx = jnp.min(jnp.where(x == mx, iota, n), axis=axis, keepdims=True)
    return mx, idx, iota == idx


def _router_kernel(lg_ref, b_ref, idx_ref, wgt_ref):
    n_e, tm = lg_ref.shape
    per = n_e // N_EXPERT_GROUPS
    scores = jax.nn.sigmoid(lg_ref[...])
    sel = scores + b_ref[...]
    neg = -jnp.inf
    grp = sel.reshape(N_EXPERT_GROUPS, per, tm)
    iota_e = lax.broadcasted_iota(jnp.int32, grp.shape, 1)
    m1, _, first = _first_max(grp, 1, iota_e)
    m2 = jnp.max(jnp.where(first, neg, grp), axis=1, keepdims=True)
    gscore = m1 + m2
    iota_g = lax.broadcasted_iota(jnp.int32, gscore.shape, 0)
    gmask = jnp.zeros(gscore.shape, F32)
    for _ in range(TOPK_GROUPS):
        _, _, hit = _first_max(gscore, 0, iota_g)
        gmask = jnp.where(hit, 1.0, gmask)
        gscore = jnp.where(hit, neg, gscore)
    cand = jnp.where(jnp.broadcast_to(gmask, grp.shape) > 0.0, grp, neg).reshape(n_e, tm)
    iota_x = lax.broadcasted_iota(jnp.int32, cand.shape, 0)
    chosen = jnp.zeros(cand.shape, F32)
    for r in range(TOP_K):
        _, idx, hit = _first_max(cand, 0, iota_x)
        chosen = jnp.where(hit, 1.0, chosen)
        cand = jnp.where(hit, neg, cand)
        idx_ref[r:r + 1, :] = idx
        wgt_ref[r:r + 1, :] = jnp.sum(jnp.where(hit, scores, 0.0), axis=0, keepdims=True)
    norm = ROUTED_SCALE / jnp.sum(chosen * scores, axis=0, keepdims=True)
    wgt_ref[...] = wgt_ref[...] * norm


def route(logits_t, router_b):
    n_e, m = logits_t.shape
    tm = _divisor(m, 512, V7X_LANES)
    spec = lambda r: pl.BlockSpec((r, tm), lambda i: (0, i))
    return pl.pallas_call(
        _router_kernel,
        grid=(m // tm,),
        in_specs=[spec(n_e), pl.BlockSpec((n_e, 1), lambda i: (0, 0))],
        out_specs=[spec(TOP_K), spec(TOP_K)],
        out_shape=[jax.ShapeDtypeStruct((TOP_K, m), jnp.int32), jax.ShapeDtypeStruct((TOP_K, m), F32)],
        compiler_params=_params(("parallel",)),
        name="moe_route",
    )(logits_t, router_b.reshape(n_e, 1))


MOE_TILE = 256


def _rank_kernel(eidx_ref, rank_ref, cnt_ref):
    @pl.when(pl.program_id(0) == 0)
    def _():
        cnt_ref[...] = jnp.zeros(cnt_ref.shape, F32)

    n_e = cnt_ref.shape[0]
    tm = eidx_ref.shape[1]
    earlier = (lax.broadcasted_iota(jnp.int32, (tm, tm), 0) < lax.broadcasted_iota(jnp.int32, (tm, tm), 1))
    upper = jnp.where(earlier, 1.0, 0.0).astype(BF16)
    iota_e = lax.broadcasted_iota(jnp.int32, (n_e, tm), 0)
    base = cnt_ref[...]
    for r in range(TOP_K):
        onehot = jnp.where(iota_e == eidx_ref[r:r + 1, :], 1.0, 0.0)
        before = jnp.dot(onehot.astype(BF16), upper, preferred_element_type=F32)
        rank_ref[r:r + 1, :] = jnp.sum(onehot * (base + before), axis=0, keepdims=True).astype(jnp.int32)
        base = base + jnp.sum(onehot, axis=1, keepdims=True)
    cnt_ref[...] = base


def moe_rank(eidx, n_e):
    _, m = eidx.shape
    tm = _divisor(m, 256, V7X_LANES)
    return pl.pallas_call(
        _rank_kernel,
        grid=(m // tm,),
        in_specs=[pl.BlockSpec((TOP_K, tm), lambda i: (0, i))],
        out_specs=[pl.BlockSpec((TOP_K, tm), lambda i: (0, i)), pl.BlockSpec((n_e, 1), lambda i: (0, 0))],
        out_shape=[jax.ShapeDtypeStruct((TOP_K, m), jnp.int32), jax.ShapeDtypeStruct((n_e, 1), F32)],
        compiler_params=_params(("arbitrary",)),
        name="moe_rank",
    )(eidx)


def _foreach(lo, hi, fn):
    def body(j, carry):
        fn(j)
        return carry

    lax.fori_loop(lo, hi, body, 0)


def _dispatch_kernel(pad_ref, pos_ref, x_ref, xs_ref, zero_ref, sem, zsem, *, n_e):
    tm = x_ref.shape[0]
    tr = zero_ref.shape[0]
    n_tiles = xs_ref.shape[0] // tr

    def zero_row_copy(row):
        return pltpu.make_async_copy(zero_ref.at[pl.ds(0, 1)], xs_ref.at[pl.ds(row, 1)], zsem)

    def zero_tile_copy(tile):
        return pltpu.make_async_copy(zero_ref, xs_ref.at[pl.ds(tile * tr, tr)], zsem)

    @pl.when(pl.program_id(0) == 0)
    def _():
        zero_ref[...] = jnp.zeros(zero_ref.shape, zero_ref.dtype)

        first_unused = pad_ref[2 * n_e]

        def start_rows(e):
            _foreach(0, pad_ref[n_e + e], lambda j: zero_row_copy(pad_ref[e] + j).start())

        def wait_rows(e):
            _foreach(0, pad_ref[n_e + e], lambda j: zero_row_copy(0).wait())

        def fill_expert(e):
            start_rows(e)
            wait_rows(e - 1)

        start_rows(0)
        _foreach(1, n_e, fill_expert)
        wait_rows(n_e - 1)
        _foreach(first_unused, n_tiles, lambda t: zero_tile_copy(t).start())
        _foreach(first_unused, n_tiles, lambda t: zero_tile_copy(0).wait())

    def scatter_token(n):
        for r in range(TOP_K):
            pltpu.make_async_copy(x_ref.at[pl.ds(n, 1)], xs_ref.at[pl.ds(pos_ref[r, n], 1)], sem).start(priority=r % 2)

    _foreach(0, tm, scatter_token)
    for r in range(TOP_K):
        pltpu.make_async_copy(x_ref, xs_ref.at[pl.ds(0, tm)], sem).wait()


def moe_dispatch(hp, pos, pad_info, n_rows, n_e):
    m, w = hp.shape
    tm = _divisor(m, 256, V7X_LANES)
    grid_spec = pltpu.PrefetchScalarGridSpec(
        num_scalar_prefetch=1,
        grid=(m // tm,),
        in_specs=[pl.BlockSpec((TOP_K, tm), lambda i, pad: (0, i), memory_space=pltpu.SMEM),
                  pl.BlockSpec((tm, w), lambda i, pad: (i, 0))],
        out_specs=pl.BlockSpec(memory_space=pl.ANY),
        scratch_shapes=[pltpu.VMEM((MOE_TILE, w), jnp.uint32), pltpu.SemaphoreType.DMA(()),
                        pltpu.SemaphoreType.DMA(())],
    )
    return pl.pallas_call(
        functools.partial(_dispatch_kernel, n_e=n_e),
        grid_spec=grid_spec,
        out_shape=jax.ShapeDtypeStruct((n_rows, w), jnp.uint32),
        compiler_params=_params(("arbitrary",)),
        name="moe_dispatch",
    )(pad_info, pos, hp)


def _moe_ffn_kernel(te_ref, nv_ref, xs_ref, wg_ref, wu_ref, wd_ref, ys_ref, wgb_ref, wub_ref, wdb_ref):
    i = pl.program_id(0)
    nv = nv_ref[i]
    new_expert = jnp.logical_or(i == 0, te_ref[i] != te_ref[jnp.maximum(i - 1, 0)])

    @pl.when(jnp.logical_and(new_expert, nv > 0))
    def _():
        wgb_ref[...] = wg_ref[...].astype(BF16)
        wub_ref[...] = wu_ref[...].astype(BF16)
        wdb_ref[...] = wd_ref[...].astype(BF16)

    @pl.when(nv > 0)
    def _():
        rows = lax.broadcasted_iota(jnp.int32, (xs_ref.shape[0], 1), 0)
        packed = jnp.where(rows < nv, xs_ref[...], jnp.uint32(0))
        lo, hi = _unpack_halves(packed)
        x = jnp.concatenate([lo.astype(BF16), hi.astype(BF16)], axis=1)
        g = jnp.dot(x, wgb_ref[...], preferred_element_type=F32)
        u = jnp.dot(x, wub_ref[...], preferred_element_type=F32)
        hid = (g * jax.nn.sigmoid(g) * u).astype(BF16)
        y = jnp.dot(hid, wdb_ref[...], preferred_element_type=F32)
        half = y.shape[1] // 2
        ys_ref[...] = _pack_halves(y[:, :half], y[:, half:])

    @pl.when(nv == 0)
    def _():
        ys_ref[...] = jnp.zeros(ys_ref.shape, jnp.uint32)


def moe_ffn(xs, tile_expert, tile_rows, w_gate, w_up, w_down, layer):
    n_rows, w = xs.shape
    _, _, d, ff = w_gate.shape
    tr = MOE_TILE
    grid_spec = pltpu.PrefetchScalarGridSpec(
        num_scalar_prefetch=2,
        grid=(n_rows // tr,),
        in_specs=[pl.BlockSpec((tr, w), lambda i, te, nv: (i, 0)),
                  pl.BlockSpec((None, None, d, ff), lambda i, te, nv: (layer, te[i], 0, 0)),
                  pl.BlockSpec((None, None, d, ff), lambda i, te, nv: (layer, te[i], 0, 0)),
                  pl.BlockSpec((None, None, ff, d), lambda i, te, nv: (layer, te[i], 0, 0))],
        out_specs=pl.BlockSpec((tr, w), lambda i, te, nv: (i, 0)),
        scratch_shapes=[pltpu.VMEM((d, ff), BF16), pltpu.VMEM((d, ff), BF16), pltpu.VMEM((ff, d), BF16)],
    )
    return pl.pallas_call(
        _moe_ffn_kernel,
        grid_spec=grid_spec,
        out_shape=jax.ShapeDtypeStruct((n_rows, w), jnp.uint32),
        compiler_params=_params(("arbitrary",)),
        name="moe_ffn",
    )(tile_expert, tile_rows, xs, w_gate, w_up, w_down)


def _combine_kernel(pos_ref, w_ref, sh_ref, ys_ref, o_ref, buf_ref, sem, *, cw):
    tm, half = buf_ref.shape[1], buf_ref.shape[2]

    def gather_token(n):
        for r in range(TOP_K):
            pltpu.make_async_copy(ys_ref.at[pl.ds(pos_ref[r, n], 1)], buf_ref.at[r, pl.ds(n, 1)],
                                  sem).start(priority=r % 2)

    _foreach(0, tm, gather_token)
    for r in range(TOP_K):
        pltpu.make_async_copy(ys_ref.at[pl.ds(0, tm)], buf_ref.at[r], sem).wait()
    for c0 in range(0, half, cw):
        acc_lo = sh_ref[:, c0:c0 + cw].astype(F32)
        acc_hi = sh_ref[:, half + c0:half + c0 + cw].astype(F32)
        for r in range(TOP_K):
            lo, hi = _unpack_halves(buf_ref[r, :, c0:c0 + cw])
            wr = w_ref[:, r:r + 1]
            acc_lo = acc_lo + wr * lo
            acc_hi = acc_hi + wr * hi
        o_ref[:, c0:c0 + cw] = acc_lo.astype(o_ref.dtype)
        o_ref[:, half + c0:half + c0 + cw] = acc_hi.astype(o_ref.dtype)


def moe_combine(ys, pos, wgt_t, shared):
    m, d = shared.shape
    w = ys.shape[1]
    tm = _divisor(m, 128, V7X_LANES)
    return pl.pallas_call(
        functools.partial(_combine_kernel, cw=_divisor(w, 256, V7X_LANES)),
        grid=(m // tm,),
        in_specs=[pl.BlockSpec((TOP_K, tm), lambda i: (0, i), memory_space=pltpu.SMEM),
                  pl.BlockSpec((tm, TOP_K), lambda i: (i, 0)),
                  pl.BlockSpec((tm, d), lambda i: (i, 0)),
                  pl.BlockSpec(memory_space=pl.ANY)],
        out_specs=pl.BlockSpec((tm, d), lambda i: (i, 0)),
        out_shape=jax.ShapeDtypeStruct((m, d), BF16),
        scratch_shapes=[pltpu.VMEM((TOP_K, tm, w), jnp.uint32), pltpu.SemaphoreType.DMA(())],
        compiler_params=_params(("arbitrary",)),
        name="moe_combine",
    )(pos, wgt_t, shared, ys)


def _ffn_up_kernel(x_ref, wg_ref, wu_ref, o_ref):
    x = x_ref[...]
    g = jnp.dot(x, wg_ref[...], preferred_element_type=F32)
    u = jnp.dot(x, wu_ref[...], preferred_element_type=F32)
    o_ref[...] = (g * jax.nn.sigmoid(g) * u).astype(o_ref.dtype)


def ffn_up(h, w_gate, w_up, layer):
    m, d = h.shape
    ff = w_gate.shape[2]
    tm = _divisor(m, 768, 16)
    wspec = pl.BlockSpec((None, d, ff), lambda i: (layer, 0, 0))
    return pl.pallas_call(
        _ffn_up_kernel,
        grid=(m // tm,),
        in_specs=[pl.BlockSpec((tm, d), lambda i: (i, 0)), wspec, wspec],
        out_specs=pl.BlockSpec((tm, ff), lambda i: (i, 0)),
        out_shape=jax.ShapeDtypeStruct((m, ff), BF16),
        compiler_params=_params(("parallel",)),
        name="ffn_up",
    )(h, w_gate, w_up)


def _pos_kernel(eidx_ref, rank_ref, start_ref, pos_ref):
    iota_e = lax.broadcasted_iota(jnp.int32, (start_ref.shape[0], eidx_ref.shape[1]), 0)
    for r in range(TOP_K):
        first_row = jnp.sum(jnp.where(iota_e == eidx_ref[r:r + 1, :], start_ref[...], 0), axis=0, keepdims=True)
        pos_ref[r:r + 1, :] = rank_ref[r:r + 1, :] + first_row


def moe_pos(eidx, rank, row_start):
    _, m = eidx.shape
    n_e = row_start.shape[0]
    tm = _divisor(m, 512, V7X_LANES)
    spec = pl.BlockSpec((TOP_K, tm), lambda i: (0, i))
    return pl.pallas_call(
        _pos_kernel,
        grid=(m // tm,),
        in_specs=[spec, spec, pl.BlockSpec((n_e, 1), lambda i: (0, 0))],
        out_specs=spec,
        out_shape=jax.ShapeDtypeStruct((TOP_K, m), jnp.int32),
        compiler_params=_params(("parallel",)),
        name="moe_pos",
    )(eidx, rank, row_start.reshape(n_e, 1))


def moe_sparse(h2, h2p, eidx, wgt, w_gate, w_up, w_down, ws_gate, ws_up, ws_down, layer):
    m, d = h2.shape
    n_e = w_gate.shape[1]
    tr = MOE_TILE
    rank, cnt = moe_rank(eidx, n_e)
    counts = cnt[:, 0].astype(jnp.int32)
    tiles_per = (counts + (tr - 1)) // tr
    tile_end = jnp.cumsum(tiles_per)
    tile_start = tile_end - tiles_per
    pos = moe_pos(eidx, rank, tile_start * tr)
    n_tiles = (TOP_K * m) // tr + n_e
    t_idx = jnp.arange(n_tiles, dtype=jnp.int32)
    tile_expert = jnp.minimum(jnp.sum((tile_end[None, :] <= t_idx[:, None]).astype(jnp.int32), axis=1), n_e - 1)
    owner = tile_expert[:, None] == jnp.arange(n_e, dtype=jnp.int32)[None, :]
    left = jnp.sum(jnp.where(owner, counts[None, :] - (t_idx[:, None] - tile_start[None, :]) * tr, 0), axis=1)
    tile_rows = jnp.clip(left, 0, tr).astype(jnp.int32)
    pad_info = jnp.concatenate([tile_start * tr + counts, tiles_per * tr - counts, tile_end[-1:]]).astype(jnp.int32)
    xs = moe_dispatch(h2p, pos, pad_info, n_tiles * tr, n_e)
    ys = moe_ffn(xs, tile_expert, tile_rows, w_gate, w_up, w_down, layer)
    shared = matmul(ffn_up(h2, ws_gate, ws_up, layer), ws_down, layer)
    return moe_combine(ys, pos, wgt.T, shared)


def _router_split(router_w):
    rw_t = router_w.T
    rw_hi = rw_t.astype(BF16)
    rw_lo = (rw_t - rw_hi.astype(F32)).astype(BF16)
    return rw_hi, rw_lo


def kernel(x, c, ctx, c_ctx, ada_w, ada_b, norm_g, w_in, lam_vec, att_g, conv_w,
           ssm_a_re, ssm_a_im, ssm_log_dt, ssm_b_re, ssm_b_im, ssm_c_re, ssm_c_im, ssm_d,
           glu_w, glu_b, p_att, p_conv, p_ssm, w_o, router_w, router_b,
           exp_w_gate, exp_w_up, exp_w_down, sh_w_gate, sh_w_up, sh_w_down):
    bsz, n_lat, d = x.shape
    n_ctx = ctx.shape[1]
    t = n_ctx + n_lat
    m = bsz * t
    depth = ada_w.shape[0]
    att_w = p_att.shape[1]
    heads = att_w // ATT_VDIM
    conv_wd = conv_w.shape[-1]
    ssm_wd = ssm_d.shape[-1]
    n_e = router_w.shape[-1]
    n_p = ssm_a_re.shape[-1]
    assert bsz + 1 <= 8 and n_ctx % SSM_CHUNK == 0 and n_lat % SSM_CHUNK == 0 and n_lat % GRID_W == 0
    assert V7X_LANES % n_p == 0 and n_e % N_EXPERT_GROUPS == 0
    k_off = att_w
    v_off = 2 * att_w
    cb_off = 3 * att_w
    su_off = cb_off + 3 * conv_wd
    g_off = su_off + ssm_wd

    stream = jnp.concatenate([ctx, x], axis=1)
    cond_cols = jnp.zeros((d, 8), F32).at[:, :bsz].set(c.T).at[:, bsz].set(c_ctx)
    tables = rope_tables(n_ctx, n_lat)
    ssm_groups = ssm_a_re.shape[2]
    perm = chunk_permutation(ssm_wd // ssm_groups)
    w_in_b, p_att_b, p_conv_b, p_ssm_b, w_o_b, glu_w_b, ws_gate_b, ws_up_b, ws_down_b = (
        a.astype(BF16) for a in (w_in, p_att, p_conv, p_ssm, w_o, glu_w, sh_w_gate, sh_w_up, sh_w_down))

    pending = None
    for i in range(depth):
        lam_init = 0.8 - 0.6 * math.exp(-0.3 * i)
        mod8 = modulation(cond_cols, ada_w, ada_b, bsz + 1, i).reshape(8, N_MOD, d)
        mod = jnp.stack([jnp.broadcast_to(mod8[bsz], (bsz, N_MOD, d)), mod8[:bsz]], axis=1)
        gains = norm_g[i]

        if pending is None:
            (h,) = resid_adaln(stream, None, mod, gains, ctx_len=n_ctx, pre_idx=0, shift_idx=0, scale_idx=1)
        else:
            branch, p_mod, p_gains = pending
            stream, h = resid_adaln(stream, branch, jnp.concatenate([mod, p_mod], axis=2),
                                    jnp.concatenate([gains, p_gains], axis=0), ctx_len=n_ctx,
                                    post_idx=4 + 3, gate_idx=N_MOD + 5, pre_idx=0, shift_idx=0, scale_idx=1)
        z = matmul(h.reshape(m, d), w_in_b, i)
        z3 = z.reshape(bsz, t, -1)

        qr, kr = rope_qk(z3, tables, att_w, n_ctx)
        att = jnp.zeros((bsz, t, att_w), BF16)
        att = diff_attention(qr, kr, z3, lam_vec[i], att_g[i], v_col_blk=v_off // ATT_VDIM, q_row0=0, n_q=n_ctx,
                             n_kv=n_ctx, lam_init=lam_init, out=att, heads=heads)
        att = diff_attention(qr, kr, z3, lam_vec[i], att_g[i], v_col_blk=v_off // ATT_VDIM, q_row0=n_ctx, n_q=n_lat,
                             n_kv=t, lam_init=lam_init, out=att, heads=heads)

        conv = short_conv(z3, conv_w[i], cb_off=cb_off, ctx_len=n_ctx)

        mats = ssm_matrices(ssm_a_re[i], ssm_a_im[i], ssm_log_dt[i], ssm_b_re[i], ssm_b_im[i],
                            ssm_c_re[i], ssm_c_im[i], ssm_d[i], V7X_LANES // n_p)
        ug = ssm_chunk_layout(z, perm, col_off=su_off, width=ssm_wd, n_g=ssm_groups)
        yg = ssm_branch(ug, mats, bsz=bsz, ctx_len=n_ctx)
        ssm = s5_glu(ssm_unchunk(yg, perm.T), glu_w_b, glu_b[i], i)

        merged = branch_merge(att.reshape(m, att_w), conv.reshape(m, conv_wd), ssm, z,
                              p_att_b, p_conv_b, p_ssm_b, i, g_off=g_off)
        o = matmul(merged, w_o_b, i).reshape(bsz, t, d)

        stream, h2, h2p, logits_t = resid_adaln(stream, o, mod, gains, ctx_len=n_ctx, post_idx=1, gate_idx=2,
                                                pre_idx=2, shift_idx=3, scale_idx=4,
                                                router=_router_split(router_w[i]))
        eidx, wgt = route(logits_t, router_b[i])
        f = moe_sparse(h2.reshape(m, d), h2p.reshape(m, d // 2), eidx, wgt, exp_w_gate, exp_w_up, exp_w_down,
                       ws_gate_b, ws_up_b, ws_down_b, i).reshape(bsz, t, d)
        pending = (f, mod, gains)

    branch, p_mod, p_gains = pending
    (out,) = resid_adaln(stream, branch, p_mod, p_gains, ctx_len=n_ctx, post_idx=3, gate_idx=5,
                         latent_only_out=True)
    return out
```

```python
import functools
import math

import jax
import jax.numpy as jnp
from jax import lax
from jax.experimental import pallas as pl
from jax.experimental.pallas import tpu as pltpu

F32 = jnp.float32
BF16 = jnp.bfloat16

GRID_W = 64
ROPE_BASE = 10000.0
ATT_DIM = 128
ATT_VDIM = 2 * ATT_DIM
N_BRANCH = 3
TOP_K = 8
N_EXPERT_GROUPS = 8
TOPK_GROUPS = 4
ROUTED_SCALE = 2.5
N_MOD = 6
EPS = 1e-6
SSM_CHUNK = 16
ATT_MAX_UNROLL = 16
ATT_KV_CHUNK = 1024
ATT_Q_ROWS = 256
SSM_LAYOUT_ROWS = 8448

V7X_LANES = 128
V7X_VMEM_LIMIT_BYTES = 56 * 1024 * 1024
ROW_BLOCK = 256


def _divisor(n, target, mult):
    best = None
    for d in range(mult, min(n, target) + 1, mult):
        if n % d == 0:
            best = d
    return best if best is not None else n


def _params(sem):
    return pltpu.CompilerParams(dimension_semantics=sem, vmem_limit_bytes=V7X_VMEM_LIMIT_BYTES)


def _mod_kernel(s_ref, w_ref, b_ref, o_ref, *, rows, kc):
    s = s_ref[...]
    s = s * jax.nn.sigmoid(s)
    o_ref[...] = jnp.zeros(o_ref.shape, F32)
    d = w_ref.shape[0]
    for r in range(rows):
        acc = b_ref[...]
        for k0 in range(0, d, kc):
            acc = acc + jnp.sum(w_ref[k0:k0 + kc, :] * s[k0:k0 + kc, r:r + 1], axis=0, keepdims=True)
        o_ref[r:r + 1, :] = acc


def modulation(cond_cols, w, b, rows, layer):
    n_l, d, n = w.shape
    tn = _divisor(n, 512, V7X_LANES)
    kc = _divisor(d, 512, 8)
    return pl.pallas_call(
        functools.partial(_mod_kernel, rows=rows, kc=kc),
        grid=(n // tn,),
        in_specs=[pl.BlockSpec((d, 8), lambda j: (0, 0)),
                  pl.BlockSpec((None, d, tn), lambda j: (layer, 0, j)),
                  pl.BlockSpec((None, 1, tn), lambda j: (layer, 0, j))],
        out_specs=pl.BlockSpec((8, tn), lambda j: (0, j)),
        out_shape=jax.ShapeDtypeStruct((8, n), F32),
        compiler_params=_params(("arbitrary",)),
        name="modulation",
    )(cond_cols, w, b.reshape(n_l, 1, n))


def _rms(x, g):
    return x * lax.rsqrt(jnp.mean(x * x, axis=-1, keepdims=True) + EPS) * g


def _pack_halves(a, b):
    lo = lax.bitcast_convert_type(a.astype(BF16).astype(F32), jnp.uint32) >> 16
    hi = lax.bitcast_convert_type(b.astype(BF16).astype(F32), jnp.uint32) & jnp.uint32(0xFFFF0000)
    return lo | hi


def _unpack_halves(p):
    lo = lax.bitcast_convert_type(p << 16, F32)
    hi = lax.bitcast_convert_type(p & jnp.uint32(0xFFFF0000), F32)
    return lo, hi


def _resid_adaln_kernel(*refs, has_branch, has_h, has_router, post_idx, gate_idx, pre_idx, shift_idx, scale_idx,
                        split_ctx_blocks):
    it = iter(refs)
    x_ref = next(it)
    xl_ref = next(it) if split_ctx_blocks else None
    o_ref = next(it) if has_branch else None
    mod_ref = next(it)
    g_ref = next(it)
    rwh_ref = next(it) if has_router else None
    rwl_ref = next(it) if has_router else None
    xout_ref = next(it) if has_branch else None
    h_ref = next(it) if has_h else None
    hp_ref = next(it) if has_router else None
    lg_ref = next(it) if has_router else None

    x = x_ref[...]
    if split_ctx_blocks:
        x = jnp.where(pl.program_id(1) < split_ctx_blocks, x, xl_ref[...])
    if has_branch:
        o = o_ref[...].astype(F32)
        x = x + mod_ref[gate_idx:gate_idx + 1, :] * _rms(o, g_ref[post_idx:post_idx + 1, :])
        xout_ref[...] = x
    if has_h:
        h = _rms(x, g_ref[pre_idx:pre_idx + 1, :])
        h = h * (1.0 + mod_ref[scale_idx:scale_idx + 1, :]) + mod_ref[shift_idx:shift_idx + 1, :]
        h_ref[...] = h.astype(BF16)
        if has_router:
            h_hi = h.astype(BF16)
            h_lo = (h - h_hi.astype(F32)).astype(BF16)
            nt = (((1,), (1,)), ((), ()))
            lg = lax.dot_general(rwh_ref[...], h_hi, nt, preferred_element_type=F32)
            lg = lg + lax.dot_general(rwh_ref[...], h_lo, nt, preferred_element_type=F32)
            lg = lg + lax.dot_general(rwl_ref[...], h_hi, nt, preferred_element_type=F32)
            lg_ref[...] = lg
            half = h.shape[1] // 2
            hp_ref[...] = _pack_halves(h[:, :half], h[:, half:])


def resid_adaln(x, branch, mod, gains, *, ctx_len, post_idx=0, gate_idx=0, pre_idx=None,
                shift_idx=0, scale_idx=0, router=None, latent_only_out=False):
    split = isinstance(x, tuple)
    bsz, t, d = (x[0].shape[0], x[0].shape[1] + x[1].shape[1], x[0].shape[2]) if split else x.shape
    rb = _divisor(math.gcd(ctx_len, t - ctx_len), ROW_BLOCK, 8)
    nctx = ctx_len // rb
    has_branch = branch is not None
    has_h = pre_idx is not None
    has_router = router is not None
    assert not (split and latent_only_out)
    off = nctx if latent_only_out else 0
    nblk = t // rb - off

    def row_map(b, i):
        return (b, i + off, 0)

    if split:
        in_specs = [pl.BlockSpec((None, rb, d), lambda b, i: (b, jnp.minimum(i, nctx - 1), 0)),
                    pl.BlockSpec((None, rb, d), lambda b, i: (b, jnp.maximum(i - nctx, 0), 0))]
        args = list(x)
    else:
        in_specs = [pl.BlockSpec((None, rb, d), row_map)]
        args = [x]
    if has_branch:
        in_specs.append(pl.BlockSpec((None, rb, d), row_map))
        args.append(branch)
    in_specs.append(pl.BlockSpec((None, None, mod.shape[2], d),
                                 lambda b, i: (b, jnp.where(i + off >= nctx, 1, 0), 0, 0)))
    args.append(mod)
    in_specs.append(pl.BlockSpec(gains.shape, lambda b, i: (0, 0)))
    args.append(gains)
    if has_router:
        rw_hi, rw_lo = router
        in_specs += [pl.BlockSpec(rw_hi.shape, lambda b, i: (0, 0)), pl.BlockSpec(rw_lo.shape, lambda b, i: (0, 0))]
        args += [rw_hi, rw_lo]
    out_specs, out_shape = [], []
    if has_branch:
        out_specs.append(pl.BlockSpec((None, rb, d), lambda b, i: (b, i, 0)))
        out_shape.append(jax.ShapeDtypeStruct((bsz, nblk * rb, d), F32))
    if has_h:
        out_specs.append(pl.BlockSpec((None, rb, d), lambda b, i: (b, i, 0)))
        out_shape.append(jax.ShapeDtypeStruct((bsz, t, d), BF16))
    if has_router:
        n_e = router[0].shape[0]
        out_specs.append(pl.BlockSpec((None, rb, d // 2), lambda b, i: (b, i, 0)))
        out_shape.append(jax.ShapeDtypeStruct((bsz, t, d // 2), jnp.uint32))
        out_specs.append(pl.BlockSpec((n_e, rb), lambda b, i: (0, b * nblk + i)))
        out_shape.append(jax.ShapeDtypeStruct((n_e, bsz * t), F32))
    return pl.pallas_call(
        functools.partial(_resid_adaln_kernel, has_branch=has_branch, has_h=has_h, has_router=has_router,
                          post_idx=post_idx, gate_idx=gate_idx, pre_idx=pre_idx, shift_idx=shift_idx,
                          scale_idx=scale_idx, split_ctx_blocks=nctx if split else 0),
        grid=(bsz, nblk),
        in_specs=in_specs,
        out_specs=out_specs,
        out_shape=out_shape,
        compiler_params=_params(("parallel", "parallel")),
        name="resid_adaln",
    )(*args)


def _mm_kernel(a_ref, b_ref, o_ref):
    o_ref[...] = jnp.dot(a_ref[...], b_ref[...], preferred_element_type=F32).astype(o_ref.dtype)


def matmul(a, b, layer, out_dtype=BF16, tm_target=768, tn_target=1024):
    m, k = a.shape
    _, _, n = b.shape
    tm = _divisor(m, tm_target, 16)
    tn = _divisor(n, tn_target, V7X_LANES)
    return pl.pallas_call(
        _mm_kernel,
        grid=(m // tm, n // tn),
        in_specs=[pl.BlockSpec((tm, k), lambda i, j: (i, 0)),
                  pl.BlockSpec((None, k, tn), lambda i, j: (layer, 0, j))],
        out_specs=pl.BlockSpec((tm, tn), lambda i, j: (i, j)),
        out_shape=jax.ShapeDtypeStruct((m, n), out_dtype),
        compiler_params=_params(("parallel", "parallel")),
        name="matmul",
    )(a, b)


def rope_tables(ctx_len, n_lat):
    nf = ATT_DIM // 4
    rows = n_lat // GRID_W
    row = jnp.repeat(jnp.arange(rows), GRID_W).astype(F32)
    col = jnp.tile(jnp.arange(GRID_W), rows).astype(F32)
    inv = ROPE_BASE ** (-jnp.arange(nf, dtype=F32) / nf)
    ang_r = row[:, None] * inv
    ang_c = col[:, None] * inv
    zero = jnp.zeros_like(ang_r)
    cos = jnp.concatenate([jnp.cos(ang_r), jnp.cos(ang_r), jnp.cos(ang_c), jnp.cos(ang_c)], axis=-1)
    sa = jnp.concatenate([-jnp.sin(ang_r), zero, -jnp.sin(ang_c), zero], axis=-1)
    sb = jnp.concatenate([zero, jnp.sin(ang_r), zero, jnp.sin(ang_c)], axis=-1)
    pad = lambda tbl, v: jnp.concatenate([jnp.full((ctx_len, ATT_DIM), v, F32), tbl], axis=0)
    return pad(cos, 1.0), pad(sa, 0.0), pad(sb, 0.0)


def _rope_kernel(q_ref, k_ref, cos_ref, sa_ref, sb_ref, qo_ref, ko_ref, *, q_scale):
    cos, sa, sb = cos_ref[...], sa_ref[...], sb_ref[...]
    quarter = ATT_DIM // 4
    for src, dst, scale in ((q_ref, qo_ref, q_scale), (k_ref, ko_ref, 1.0)):
        for g0 in range(0, src.shape[-1], ATT_DIM):
            x = src[:, g0:g0 + ATT_DIM].astype(F32)
            y = x * cos + pltpu.roll(x, ATT_DIM - quarter, 1) * sa + pltpu.roll(x, quarter, 1) * sb
            dst[:, g0:g0 + ATT_DIM] = (y * scale).astype(dst.dtype)


def rope_qk(z3, tables, att_w, ctx_len):
    bsz, t, _ = z3.shape
    rb = _divisor(math.gcd(ctx_len, t - ctx_len), ROW_BLOCK, 8)
    tbl_spec = pl.BlockSpec((rb, ATT_DIM), lambda b, i: (i, 0))
    out_spec = pl.BlockSpec((None, rb, att_w), lambda b, i: (b, i, 0))
    return pl.pallas_call(
        functools.partial(_rope_kernel, q_scale=ATT_DIM ** -0.5 * math.log2(math.e)),
        grid=(bsz, t // rb),
        in_specs=[pl.BlockSpec((None, rb, att_w), lambda b, i: (b, i, 0)),
                  pl.BlockSpec((None, rb, att_w), lambda b, i: (b, i, 1)),
                  tbl_spec, tbl_spec, tbl_spec],
        out_specs=[out_spec, out_spec],
        out_shape=[jax.ShapeDtypeStruct((bsz, t, att_w), BF16)] * 2,
        compiler_params=_params(("parallel", "parallel")),
        name="rope_qk",
    )(z3, z3, *tables)


def _attn_kernel(lam_ref, g_ref, q_ref, k_ref, v_ref, prev_ref, o_ref, acc1_ref, acc2_ref, *, tk, lam_init):
    del prev_ref
    q = q_ref[...]
    q1, q2 = q[:, :ATT_DIM], q[:, ATT_DIM:]
    lv = lam_ref[...]
    lam = (jnp.exp(jnp.sum(lv[0:1] * lv[1:2], axis=-1, keepdims=True))
           - jnp.exp(jnp.sum(lv[2:3] * lv[3:4], axis=-1, keepdims=True)) + lam_init)
    tq = q.shape[0]
    nt = (((1,), (1,)), ((), ()))
    acc1_ref[...] = jnp.zeros(acc1_ref.shape, F32)
    acc2_ref[...] = jnp.zeros(acc2_ref.shape, F32)

    def one_map(qm, km, vc, acc_ref, m, l):
        s = lax.dot_general(qm, km, nt, preferred_element_type=F32)
        m_new = jnp.maximum(m, jnp.max(s, axis=-1, keepdims=True))
        alpha = jnp.exp2(m - m_new)
        p = jnp.exp2(s - m_new)
        l_new = alpha * l + jnp.sum(p, axis=-1, keepdims=True)
        acc_ref[...] = alpha * acc_ref[...] + jnp.dot(p.astype(BF16), vc, preferred_element_type=F32)
        return m_new, l_new

    def body(c, carry):
        m1, l1, m2, l2 = carry
        r0 = pl.multiple_of(c * tk, tk)
        kc = k_ref[pl.ds(r0, tk), :]
        vc = v_ref[pl.ds(r0, tk), :]
        m1, l1 = one_map(q1, kc[:, :ATT_DIM], vc, acc1_ref, m1, l1)
        m2, l2 = one_map(q2, kc[:, ATT_DIM:], vc, acc2_ref, m2, l2)
        return m1, l1, m2, l2

    neg = jnp.full((tq, 1), -jnp.inf, F32)
    zero = jnp.zeros((tq, 1), F32)
    carry = (neg, zero, neg, zero)
    n_chunks = k_ref.shape[0] // tk
    unroll = min(n_chunks, ATT_MAX_UNROLL)
    peeled = n_chunks % unroll
    for c in range(peeled):
        carry = body(c, carry)

    def trip(i, cr):
        for j in range(unroll):
            cr = body(peeled + unroll * i + j, cr)
        return cr

    _, l1, _, l2 = lax.fori_loop(0, n_chunks // unroll, trip, carry)
    o = acc1_ref[...] / l1 - lam * (acc2_ref[...] / l2)
    o = _rms(o, g_ref[...]) * (1.0 - lam_init)
    o_ref[...] = o.astype(o_ref.dtype)


def diff_attention(qr, kr, z3, lam_vec, att_g, *, v_col_blk, q_row0, n_q, n_kv, lam_init, out, heads):
    bsz, t, att_w = qr.shape
    tq = _divisor(math.gcd(q_row0, n_q) if q_row0 else n_q, ATT_Q_ROWS, 8)
    tk = _divisor(n_kv, ATT_KV_CHUNK, V7X_LANES)
    qoff = q_row0 // tq
    q_spec = pl.BlockSpec((None, tq, ATT_VDIM), lambda b, h, i: (b, i + qoff, h))
    return pl.pallas_call(
        functools.partial(_attn_kernel, tk=tk, lam_init=lam_init),
        grid=(bsz, heads, n_q // tq),
        in_specs=[pl.BlockSpec(lam_vec.shape, lambda b, h, i: (0, 0)),
                  pl.BlockSpec((1, ATT_VDIM), lambda b, h, i: (0, 0)),
                  q_spec,
                  pl.BlockSpec((None, n_kv, ATT_VDIM), lambda b, h, i: (b, 0, h)),
                  pl.BlockSpec((None, n_kv, ATT_VDIM), lambda b, h, i: (b, 0, v_col_blk + h)),
                  pl.BlockSpec(memory_space=pl.ANY)],
        out_specs=q_spec,
        out_shape=jax.ShapeDtypeStruct((bsz, t, att_w), BF16),
        input_output_aliases={5: 0},
        scratch_shapes=[pltpu.VMEM((tq, ATT_VDIM), F32), pltpu.VMEM((tq, ATT_VDIM), F32)],
        compiler_params=_params(("parallel", "parallel", "arbitrary")),
        name="diff_attention",
    )(lam_vec, att_g.reshape(1, ATT_VDIM), qr, kr, z3, out)


def _conv_kernel(cb_ref, cc_ref, ch_ref, w_ref, o_ref, p_ref, *, ctx_len, rc):
    t, tc = cb_ref.shape
    pad = 8
    p_ref[0:pad, :] = jnp.zeros((pad, tc), F32)
    p_ref[pad + t:pad + t + pad, :] = jnp.zeros((pad, tc), F32)
    for r0 in range(0, t, rc):
        p_ref[pad + r0:pad + r0 + rc, :] = cc_ref[r0:r0 + rc, :].astype(F32) * ch_ref[r0:r0 + rc, :].astype(F32)
    w0, w1, w2 = w_ref[0:1, :], w_ref[1:2, :], w_ref[2:3, :]
    for r0 in range(0, t, rc):
        row = r0 + lax.broadcasted_iota(jnp.int32, (rc, 1), 0)
        prev = p_ref[pad + r0 - 1:pad + r0 - 1 + rc, :]
        cur = p_ref[pad + r0:pad + r0 + rc, :]
        nxt = p_ref[pad + r0 + 1:pad + r0 + 1 + rc, :]
        prev = jnp.where(row == ctx_len, 0.0, prev)
        nxt = jnp.where(row == ctx_len - 1, 0.0, nxt)
        y = cb_ref[r0:r0 + rc, :].astype(F32) * (w0 * prev + w1 * cur + w2 * nxt)
        o_ref[r0:r0 + rc, :] = y.astype(o_ref.dtype)


def short_conv(z3, conv_w, *, cb_off, ctx_len):
    bsz, t, _ = z3.shape
    conv_wd = conv_w.shape[-1]
    tc = V7X_LANES
    rc = _divisor(t, 768, 8)
    blk0 = cb_off // tc
    nblk = conv_wd // tc
    spec = lambda k: pl.BlockSpec((None, t, tc), lambda b, j: (b, 0, blk0 + k * nblk + j))
    return pl.pallas_call(
        functools.partial(_conv_kernel, ctx_len=ctx_len, rc=rc),
        grid=(bsz, nblk),
        in_specs=[spec(0), spec(1), spec(2), pl.BlockSpec((3, tc), lambda b, j: (0, j))],
        out_specs=pl.BlockSpec((None, t, tc), lambda b, j: (b, 0, j)),
        out_shape=jax.ShapeDtypeStruct((bsz, t, conv_wd), BF16),
        scratch_shapes=[pltpu.VMEM((t + 16, tc), F32)],
        compiler_params=_params(("parallel", "parallel")),
        name="short_conv",
    )(z3, z3, z3, conv_w)


def ssm_matrices(a_re, a_im, log_dt, b_re, b_im, c_re, c_im, ssm_d, gp):
    tc = SSM_CHUNK
    n_g, n_p = a_re.shape[1], a_re.shape[2]
    n_i = b_re.shape[-1]
    dt = jnp.exp(log_dt.astype(F32))[..., None]
    lr = jnp.minimum(a_re.astype(F32), -1e-4)
    li = a_im.astype(F32)

    def power(n):
        nn = n.astype(F32)[:, None, None, None]
        mag = jnp.exp(nn * (lr * dt))
        return mag * jnp.cos(nn * (li * dt)), mag * jnp.sin(nn * (li * dt))

    ar, ai = power(jnp.ones((1,), F32))
    ar, ai = ar[0], ai[0]
    den = lr * lr + li * li
    cr = ((ar - 1.0) * lr + ai * li) / den
    ci = (ai * lr - (ar - 1.0) * li) / den
    bre, bim = b_re.astype(F32), b_im.astype(F32)
    br = cr[..., None] * bre - ci[..., None] * bim
    bi = cr[..., None] * bim + ci[..., None] * bre
    cre, cim = c_re.astype(F32), c_im.astype(F32)

    pr, pi = power(jnp.arange(tc + 1))
    wr = pr[..., None] * br - pi[..., None] * bi
    wi = pr[..., None] * bi + pi[..., None] * br
    kern = jnp.einsum('dgip,ndgpj->ndgij', cre, wr) - jnp.einsum('dgip,ndgpj->ndgij', cim, wi)

    r_idx = jnp.arange(tc)[:, None]
    s_idx = jnp.arange(tc)[None, :]

    def toeplitz(lag, d):
        blk = kern[jnp.clip(lag, 0, tc), d]
        blk = jnp.where((lag >= 0)[:, :, None, None, None], blk, 0.0)
        return blk.transpose(2, 0, 4, 1, 3).reshape(n_g, tc * n_i, tc * n_i)

    w = jnp.stack([toeplitz(s_idx - r_idx, 0), toeplitz(r_idx - s_idx, 1)])

    def lane_pad_cols(m):
        q = jnp.arange(n_g) % gp
        onehot = jax.nn.one_hot(q, gp, dtype=F32)
        return (m[:, :, None, :] * onehot[:, None, :, None]).reshape(n_g, m.shape[1], gp * n_p)

    def state_in(d, n_of_r):
        sel = n_of_r
        re = wr[sel, d].transpose(1, 0, 3, 2).reshape(n_g, tc * n_i, n_p)
        im = wi[sel, d].transpose(1, 0, 3, 2).reshape(n_g, tc * n_i, n_p)
        return jnp.stack([lane_pad_cols(re), lane_pad_cols(im)])

    we = jnp.stack([state_in(0, tc - 1 - jnp.arange(tc)), state_in(1, jnp.arange(tc))])

    def state_out(d, n_of_s):
        prs, pis = pr[n_of_s, d], pi[n_of_s, d]
        qr = cre[d][None] * prs[:, :, None, :] - cim[d][None] * pis[:, :, None, :]
        qi = cre[d][None] * pis[:, :, None, :] + cim[d][None] * prs[:, :, None, :]
        fre = qr.transpose(1, 0, 2, 3).reshape(n_g, tc * n_i, n_p)
        fim = -qi.transpose(1, 0, 2, 3).reshape(n_g, tc * n_i, n_p)
        return jnp.stack([lane_pad_cols(fre), lane_pad_cols(fim)]).transpose(0, 1, 3, 2)

    wf = jnp.stack([state_out(0, jnp.arange(tc) + 1), state_out(1, tc - jnp.arange(tc))])

    at = jnp.stack([pr[tc].reshape(2, n_g * n_p), pi[tc].reshape(2, n_g * n_p)], axis=1)
    dd = jnp.tile(ssm_d.astype(F32).reshape(n_g, 1, n_i), (1, 1, tc))
    return w.astype(BF16), we.astype(BF16), wf.astype(BF16), at, dd


def _ssm_state_kernel(u_ref, we_ref, sre_ref, sim_ref):
    gp = u_ref.shape[0]
    for d in range(2):
        for part, dst in ((0, sre_ref), (1, sim_ref)):
            acc = jnp.dot(u_ref[0], we_ref[d, part, 0], preferred_element_type=F32)
            for q in range(1, gp):
                acc = acc + jnp.dot(u_ref[q], we_ref[d, part, q], preferred_element_type=F32)
            dst[d] = acc


def _ssm_scan_kernel(sre_ref, sim_ref, at_ref, hre_ref, him_ref, *, bsz, nch, nctx):
    d = pl.program_id(0)
    atr, ati = at_ref[0:1, :], at_ref[1:2, :]
    zero = jnp.zeros(atr.shape, F32)

    def body(k, carry):
        c_rev = jnp.where(k < nctx, nctx - 1 - k, nch - 1 - (k - nctx))
        c = jnp.where(d == 0, k, c_rev)
        new = []
        for b in range(bsz):
            hr, hi = carry[2 * b], carry[2 * b + 1]
            row = pl.ds(b * nch + c, 1)
            hre_ref[row, :] = hr
            him_ref[row, :] = hi
            new.append(atr * hr - ati * hi + sre_ref[row, :])
            new.append(atr * hi + ati * hr + sim_ref[row, :])
        return tuple(new)

    lax.fori_loop(0, nch, body, tuple([zero] * (2 * bsz)))


def _ssm_out_kernel(u_ref, w_ref, wf_ref, hre_ref, him_ref, dd_ref, y_ref):
    gp = u_ref.shape[0]
    for q in range(gp):
        u = u_ref[q]
        y = u.astype(F32) * dd_ref[q]
        for d in range(2):
            y = y + jnp.dot(u, w_ref[d, q], preferred_element_type=F32)
            y = y + jnp.dot(hre_ref[d].astype(BF16), wf_ref[d, 0, q], preferred_element_type=F32)
            y = y + jnp.dot(him_ref[d].astype(BF16), wf_ref[d, 1, q], preferred_element_type=F32)
        y_ref[q] = y.astype(y_ref.dtype)


def chunk_permutation(n_i):
    tc = SSM_CHUNK
    gs = V7X_LANES // n_i
    src = jnp.arange(tc * V7X_LANES)
    r, g, j = src // V7X_LANES, (src % V7X_LANES) // n_i, src % n_i
    dst = g * (tc * n_i) + r * n_i + j
    return (dst[:, None] == jnp.arange(tc * V7X_LANES)[None, :]).astype(BF16)


def _ssm_in_kernel(z_ref, perm_ref, ug_ref, x32_ref):
    tc = SSM_CHUNK
    nchb = z_ref.shape[0] // tc
    x32_ref[...] = z_ref[...].astype(F32)
    steps = [x32_ref[pl.ds(r, nchb, stride=tc), :] for r in range(tc)]
    x = jnp.concatenate(steps, axis=1).astype(BF16)
    xp = jnp.dot(x, perm_ref[...], preferred_element_type=F32)
    wg = ug_ref.shape[2]
    for g in range(ug_ref.shape[0]):
        ug_ref[g] = xp[:, g * wg:(g + 1) * wg].astype(ug_ref.dtype)


def ssm_chunk_layout(z, perm, *, col_off, width, n_g):
    m = z.shape[0]
    tc = SSM_CHUNK
    n_i = width // n_g
    gs = V7X_LANES // n_i
    tm = _divisor(m, SSM_LAYOUT_ROWS, 16 * tc)
    return pl.pallas_call(
        _ssm_in_kernel,
        grid=(m // tm, width // V7X_LANES),
        in_specs=[pl.BlockSpec((tm, V7X_LANES), lambda i, s: (i, col_off // V7X_LANES + s)),
                  pl.BlockSpec(perm.shape, lambda i, s: (0, 0))],
        out_specs=pl.BlockSpec((gs, tm // tc, tc * n_i), lambda i, s: (s, i, 0)),
        out_shape=jax.ShapeDtypeStruct((n_g, m // tc, tc * n_i), BF16),
        scratch_shapes=[pltpu.VMEM((tm, V7X_LANES), F32)],
        compiler_params=_params(("parallel", "parallel")),
        name="ssm_chunk_layout",
    )(z, perm)


def ssm_branch(ug, mats, *, bsz, ctx_len):
    w, we, wf, at, dd = mats
    tc = SSM_CHUNK
    n_g, mc, _ = ug.shape
    n_i = ug.shape[2] // tc
    gpp = we.shape[-1]
    n_p = at.shape[-1] // n_g
    gp = gpp // n_p
    nch = mc // bsz

    s_spec = pl.BlockSpec((2, mc, gpp), lambda g: (0, 0, g))
    s_shape = jax.ShapeDtypeStruct((2, mc, n_g * n_p), F32)
    sre, sim = pl.pallas_call(
        _ssm_state_kernel,
        grid=(n_g // gp,),
        in_specs=[pl.BlockSpec((gp, mc, tc * n_i), lambda g: (g, 0, 0)),
                  pl.BlockSpec((2, 2, gp, tc * n_i, gpp), lambda g: (0, 0, g, 0, 0))],
        out_specs=[s_spec, s_spec],
        out_shape=[s_shape, s_shape],
        compiler_params=_params(("parallel",)),
        name="ssm_chunk_state",
    )(ug, we)

    lanes = _divisor(n_g * n_p, 4 * V7X_LANES, V7X_LANES)
    st_spec = pl.BlockSpec((None, mc, lanes), lambda di, j: (di, 0, j))
    hre, him = pl.pallas_call(
        functools.partial(_ssm_scan_kernel, bsz=bsz, nch=nch, nctx=ctx_len // tc),
        grid=(2, n_g * n_p // lanes),
        in_specs=[st_spec, st_spec, pl.BlockSpec((None, 2, lanes), lambda di, j: (di, 0, j))],
        out_specs=[st_spec, st_spec],
        out_shape=[s_shape, s_shape],
        compiler_params=_params(("parallel", "parallel")),
        name="ssm_chunk_scan",
    )(sre, sim, at)

    yg = pl.pallas_call(
        _ssm_out_kernel,
        grid=(n_g // gp,),
        in_specs=[pl.BlockSpec((gp, mc, tc * n_i), lambda g: (g, 0, 0)),
                  pl.BlockSpec((2, gp, tc * n_i, tc * n_i), lambda g: (0, g, 0, 0)),
                  pl.BlockSpec((2, 2, gp, gpp, tc * n_i), lambda g: (0, 0, g, 0, 0)),
                  s_spec, s_spec,
                  pl.BlockSpec((gp, 1, tc * n_i), lambda g: (g, 0, 0))],
        out_specs=pl.BlockSpec((gp, mc, tc * n_i), lambda g: (g, 0, 0)),
        out_shape=jax.ShapeDtypeStruct((n_g, mc, tc * n_i), BF16),
        compiler_params=_params(("parallel",)),
        name="ssm_chunk_out",
    )(ug, w, wf, hre, him, dd)
    return yg


def _ssm_unchunk_kernel(yg_ref, perm_t_ref, y_ref, y32_ref):
    tc = SSM_CHUNK
    gs, nchb, _ = yg_ref.shape
    yp = jnp.concatenate([yg_ref[g] for g in range(gs)], axis=1)
    ys = jnp.dot(yp, perm_t_ref[...], preferred_element_type=F32)
    for r in range(tc):
        y32_ref[pl.ds(r, nchb, stride=tc), :] = ys[:, r * V7X_LANES:(r + 1) * V7X_LANES]
    y_ref[...] = y32_ref[...].astype(y_ref.dtype)


def ssm_unchunk(yg, perm_t):
    tc = SSM_CHUNK
    n_g, mc, wg = yg.shape
    m = mc * tc
    gs = perm_t.shape[0] // wg
    tm = _divisor(m, SSM_LAYOUT_ROWS, 16 * tc)
    return pl.pallas_call(
        _ssm_unchunk_kernel,
        grid=(m // tm, n_g // gs),
        in_specs=[pl.BlockSpec((gs, tm // tc, wg), lambda i, s: (s, i, 0)),
                  pl.BlockSpec(perm_t.shape, lambda i, s: (0, 0))],
        out_specs=pl.BlockSpec((tm, V7X_LANES), lambda i, s: (i, s)),
        out_shape=jax.ShapeDtypeStruct((m, n_g * wg // tc), BF16),
        scratch_shapes=[pltpu.VMEM((tm, V7X_LANES), F32)],
        compiler_params=_params(("parallel", "parallel")),
        name="ssm_unchunk",
    )(yg, perm_t)


def _glu_kernel(y_ref, w_ref, b_ref, o_ref):
    y = y_ref[...].astype(F32)
    g = 0.5 * y * (1.0 + jnp.tanh(math.sqrt(2.0 / math.pi) * (y + 0.044715 * (y * y * y))))
    r = jnp.dot(g.astype(BF16), w_ref[...], preferred_element_type=F32) + b_ref[...]
    o_ref[...] = (g * jax.nn.sigmoid(r)).astype(o_ref.dtype)


def s5_glu(y, w, b, layer):
    m, width = y.shape
    tm = _divisor(m, 768, 16)
    return pl.pallas_call(
        _glu_kernel,
        grid=(m // tm,),
        in_specs=[pl.BlockSpec((tm, width), lambda i: (i, 0)),
                  pl.BlockSpec((None, width, width), lambda i: (layer, 0, 0)),
                  pl.BlockSpec((1, width), lambda i: (0, 0))],
        out_specs=pl.BlockSpec((tm, width), lambda i: (i, 0)),
        out_shape=jax.ShapeDtypeStruct((m, width), BF16),
        compiler_params=_params(("parallel",)),
        name="s5_glu",
    )(y, w, b.reshape(1, width))


def _merge_kernel(att_ref, conv_ref, ssm_ref, g0_ref, g1_ref, g2_ref, pa_ref, pc_ref, ps_ref, o_ref):
    def branch(x_ref, p_ref, g_ref):
        y = jnp.dot(x_ref[...], p_ref[...], preferred_element_type=F32)
        return jax.nn.sigmoid(g_ref[...].astype(F32)) * y

    m = branch(att_ref, pa_ref, g0_ref) + branch(conv_ref, pc_ref, g1_ref) + branch(ssm_ref, ps_ref, g2_ref)
    o_ref[...] = m.astype(o_ref.dtype)


def branch_merge(att, conv, ssm, z, p_att, p_conv, p_ssm, layer, *, g_off):
    m, d = att.shape[0], p_att.shape[2]
    tm = _divisor(m, 768, 16)
    tn = _divisor(math.gcd(d, g_off), 1024, V7X_LANES)
    gblk = g_off // tn
    nj = d // tn
    row = lambda a: pl.BlockSpec((tm, a.shape[1]), lambda i, j: (i, 0))
    gate = lambda k: pl.BlockSpec((tm, tn), lambda i, j: (i, gblk + k * nj + j))
    col = lambda p: pl.BlockSpec((None, p.shape[1], tn), lambda i, j: (layer, 0, j))
    return pl.pallas_call(
        _merge_kernel,
        grid=(m // tm, nj),
        in_specs=[row(att), row(conv), row(ssm), gate(0), gate(1), gate(2), col(p_att), col(p_conv), col(p_ssm)],
        out_specs=pl.BlockSpec((tm, tn), lambda i, j: (i, j)),
        out_shape=jax.ShapeDtypeStruct((m, d), BF16),
        compiler_params=_params(("parallel", "parallel")),
        name="branch_merge",
    )(att, conv, ssm, z, z, z, p_att, p_conv, p_ssm)


def _first_max(x, axis, iota):
    mx = jnp.max(x, axis=axis, keepdims=True)
    n = x.shape[axis]
    idx = jnp.min(jnp.where(x == mx, iota, n), axis=axis, keepdims=True)
    return mx, idx, iota == idx


def _router_kernel(lg_ref, b_ref, idx_ref, wgt_ref):
    n_e, tm = lg_ref.shape
    per = n_e // N_EXPERT_GROUPS
    scores = jax.nn.sigmoid(lg_ref[...])
    sel = scores + b_ref[...]
    neg = -jnp.inf
    grp = sel.reshape(N_EXPERT_GROUPS, per, tm)
    iota_e = lax.broadcasted_iota(jnp.int32, grp.shape, 1)
    m1, _, first = _first_max(grp, 1, iota_e)
    m2 = jnp.max(jnp.where(first, neg, grp), axis=1, keepdims=True)
    gscore = m1 + m2
    iota_g = lax.broadcasted_iota(jnp.int32, gscore.shape, 0)
    gmask = jnp.zeros(gscore.shape, F32)
    for _ in range(TOPK_GROUPS):
        _, _, hit = _first_max(gscore, 0, iota_g)
        gmask = jnp.where(hit, 1.0, gmask)
        gscore = jnp.where(hit, neg, gscore)
    cand = jnp.where(jnp.broadcast_to(gmask, grp.shape) > 0.0, grp, neg).reshape(n_e, tm)
    iota_x = lax.broadcasted_iota(jnp.int32, cand.shape, 0)
    chosen = jnp.zeros(cand.shape, F32)
    for r in range(TOP_K):
        _, idx, hit = _first_max(cand, 0, iota_x)
        chosen = jnp.where(hit, 1.0, chosen)
        cand = jnp.where(hit, neg, cand)
        idx_ref[r:r + 1, :] = idx
        wgt_ref[r:r + 1, :] = jnp.sum(jnp.where(hit, scores, 0.0), axis=0, keepdims=True)
    norm = ROUTED_SCALE / jnp.sum(chosen * scores, axis=0, keepdims=True)
    wgt_ref[...] = wgt_ref[...] * norm


def route(logits_t, router_b):
    n_e, m = logits_t.shape
    tm = _divisor(m, 512, V7X_LANES)
    spec = lambda r: pl.BlockSpec((r, tm), lambda i: (0, i))
    return pl.pallas_call(
        _router_kernel,
        grid=(m // tm,),
        in_specs=[spec(n_e), pl.BlockSpec((n_e, 1), lambda i: (0, 0))],
        out_specs=[spec(TOP_K), spec(TOP_K)],
        out_shape=[jax.ShapeDtypeStruct((TOP_K, m), jnp.int32), jax.ShapeDtypeStruct((TOP_K, m), F32)],
        compiler_params=_params(("parallel",)),
        name="moe_route",
    )(logits_t, router_b.reshape(n_e, 1))


MOE_TILE = 256


def _rank_kernel(eidx_ref, rank_ref, cnt_ref):
    @pl.when(pl.program_id(0) == 0)
    def _():
        cnt_ref[...] = jnp.zeros(cnt_ref.shape, F32)

    n_e = cnt_ref.shape[0]
    tm = eidx_ref.shape[1]
    earlier = (lax.broadcasted_iota(jnp.int32, (tm, tm), 0) < lax.broadcasted_iota(jnp.int32, (tm, tm), 1))
    upper = jnp.where(earlier, 1.0, 0.0).astype(BF16)
    iota_e = lax.broadcasted_iota(jnp.int32, (n_e, tm), 0)
    base = cnt_ref[...]
    for r in range(TOP_K):
        onehot = jnp.where(iota_e == eidx_ref[r:r + 1, :], 1.0, 0.0)
        before = jnp.dot(onehot.astype(BF16), upper, preferred_element_type=F32)
        rank_ref[r:r + 1, :] = jnp.sum(onehot * (base + before), axis=0, keepdims=True).astype(jnp.int32)
        base = base + jnp.sum(onehot, axis=1, keepdims=True)
    cnt_ref[...] = base


def moe_rank(eidx, n_e):
    _, m = eidx.shape
    tm = _divisor(m, 256, V7X_LANES)
    return pl.pallas_call(
        _rank_kernel,
        grid=(m // tm,),
        in_specs=[pl.BlockSpec((TOP_K, tm), lambda i: (0, i))],
        out_specs=[pl.BlockSpec((TOP_K, tm), lambda i: (0, i)), pl.BlockSpec((n_e, 1), lambda i: (0, 0))],
        out_shape=[jax.ShapeDtypeStruct((TOP_K, m), jnp.int32), jax.ShapeDtypeStruct((n_e, 1), F32)],
        compiler_params=_params(("arbitrary",)),
        name="moe_rank",
    )(eidx)


def _foreach(lo, hi, fn):
    def body(j, carry):
        fn(j)
        return carry

    lax.fori_loop(lo, hi, body, 0)


def _dispatch_kernel(pad_ref, pos_ref, x_ref, xs_ref, zero_ref, sem, zsem, *, n_e):
    tm = x_ref.shape[0]
    tr = zero_ref.shape[0]
    n_tiles = xs_ref.shape[0] // tr

    def zero_row_copy(row):
        return pltpu.make_async_copy(zero_ref.at[pl.ds(0, 1)], xs_ref.at[pl.ds(row, 1)], zsem)

    def zero_tile_copy(tile):
        return pltpu.make_async_copy(zero_ref, xs_ref.at[pl.ds(tile * tr, tr)], zsem)

    @pl.when(pl.program_id(0) == 0)
    def _():
        zero_ref[...] = jnp.zeros(zero_ref.shape, zero_ref.dtype)

        first_unused = pad_ref[2 * n_e]

        def start_rows(e):
            _foreach(0, pad_ref[n_e + e], lambda j: zero_row_copy(pad_ref[e] + j).start())

        def wait_rows(e):
            _foreach(0, pad_ref[n_e + e], lambda j: zero_row_copy(0).wait())

        def fill_expert(e):
            start_rows(e)
            wait_rows(e - 1)

        start_rows(0)
        _foreach(1, n_e, fill_expert)
        wait_rows(n_e - 1)
        _foreach(first_unused, n_tiles, lambda t: zero_tile_copy(t).start())
        _foreach(first_unused, n_tiles, lambda t: zero_tile_copy(0).wait())

    def scatter_token(n):
        for r in range(TOP_K):
            pltpu.make_async_copy(x_ref.at[pl.ds(n, 1)], xs_ref.at[pl.ds(pos_ref[r, n], 1)], sem).start(priority=r % 2)

    _foreach(0, tm, scatter_token)
    for r in range(TOP_K):
        pltpu.make_async_copy(x_ref, xs_ref.at[pl.ds(0, tm)], sem).wait()


def moe_dispatch(hp, pos, pad_info, n_rows, n_e):
    m, w = hp.shape
    tm = _divisor(m, 256, V7X_LANES)
    grid_spec = pltpu.PrefetchScalarGridSpec(
        num_scalar_prefetch=1,
        grid=(m // tm,),
        in_specs=[pl.BlockSpec((TOP_K, tm), lambda i, pad: (0, i), memory_space=pltpu.SMEM),
                  pl.BlockSpec((tm, w), lambda i, pad: (i, 0))],
        out_specs=pl.BlockSpec(memory_space=pl.ANY),
        scratch_shapes=[pltpu.VMEM((MOE_TILE, w), jnp.uint32), pltpu.SemaphoreType.DMA(()),
                        pltpu.SemaphoreType.DMA(())],
    )
    return pl.pallas_call(
        functools.partial(_dispatch_kernel, n_e=n_e),
        grid_spec=grid_spec,
        out_shape=jax.ShapeDtypeStruct((n_rows, w), jnp.uint32),
        compiler_params=_params(("arbitrary",)),
        name="moe_dispatch",
    )(pad_info, pos, hp)


def _moe_ffn_kernel(te_ref, nv_ref, xs_ref, wg_ref, wu_ref, wd_ref, ys_ref, wgb_ref, wub_ref, wdb_ref):
    i = pl.program_id(0)
    nv = nv_ref[i]
    new_expert = jnp.logical_or(i == 0, te_ref[i] != te_ref[jnp.maximum(i - 1, 0)])

    @pl.when(jnp.logical_and(new_expert, nv > 0))
    def _():
        wgb_ref[...] = wg_ref[...].astype(BF16)
        wub_ref[...] = wu_ref[...].astype(BF16)
        wdb_ref[...] = wd_ref[...].astype(BF16)

    @pl.when(nv > 0)
    def _():
        rows = lax.broadcasted_iota(jnp.int32, (xs_ref.shape[0], 1), 0)
        packed = jnp.where(rows < nv, xs_ref[...], jnp.uint32(0))
        lo, hi = _unpack_halves(packed)
        x = jnp.concatenate([lo.astype(BF16), hi.astype(BF16)], axis=1)
        g = jnp.dot(x, wgb_ref[...], preferred_element_type=F32)
        u = jnp.dot(x, wub_ref[...], preferred_element_type=F32)
        hid = (g * jax.nn.sigmoid(g) * u).astype(BF16)
        y = jnp.dot(hid, wdb_ref[...], preferred_element_type=F32)
        half = y.shape[1] // 2
        ys_ref[...] = _pack_halves(y[:, :half], y[:, half:])

    @pl.when(nv == 0)
    def _():
        ys_ref[...] = jnp.zeros(ys_ref.shape, jnp.uint32)


def moe_ffn(xs, tile_expert, tile_rows, w_gate, w_up, w_down, layer):
    n_rows, w = xs.shape
    _, _, d, ff = w_gate.shape
    tr = MOE_TILE
    grid_spec = pltpu.PrefetchScalarGridSpec(
        num_scalar_prefetch=2,
        grid=(n_rows // tr,),
        in_specs=[pl.BlockSpec((tr, w), lambda i, te, nv: (i, 0)),
                  pl.BlockSpec((None, None, d, ff), lambda i, te, nv: (layer, te[i], 0, 0)),
                  pl.BlockSpec((None, None, d, ff), lambda i, te, nv: (layer, te[i], 0, 0)),
                  pl.BlockSpec((None, None, ff, d), lambda i, te, nv: (layer, te[i], 0, 0))],
        out_specs=pl.BlockSpec((tr, w), lambda i, te, nv: (i, 0)),
        scratch_shapes=[pltpu.VMEM((d, ff), BF16), pltpu.VMEM((d, ff), BF16), pltpu.VMEM((ff, d), BF16)],
    )
    return pl.pallas_call(
        _moe_ffn_kernel,
        grid_spec=grid_spec,
        out_shape=jax.ShapeDtypeStruct((n_rows, w), jnp.uint32),
        compiler_params=_params(("arbitrary",)),
        name="moe_ffn",
    )(tile_expert, tile_rows, xs, w_gate, w_up, w_down)


def _combine_kernel(pos_ref, w_ref, sh_ref, ys_ref, o_ref, buf_ref, sem, *, cw):
    tm, half = buf_ref.shape[1], buf_ref.shape[2]

    def gather_token(n):
        for r in range(TOP_K):
            pltpu.make_async_copy(ys_ref.at[pl.ds(pos_ref[r, n], 1)], buf_ref.at[r, pl.ds(n, 1)],
                                  sem).start(priority=r % 2)

    _foreach(0, tm, gather_token)
    for r in range(TOP_K):
        pltpu.make_async_copy(ys_ref.at[pl.ds(0, tm)], buf_ref.at[r], sem).wait()
    for c0 in range(0, half, cw):
        acc_lo = sh_ref[:, c0:c0 + cw].astype(F32)
        acc_hi = sh_ref[:, half + c0:half + c0 + cw].astype(F32)
        for r in range(TOP_K):
            lo, hi = _unpack_halves(buf_ref[r, :, c0:c0 + cw])
            wr = w_ref[:, r:r + 1]
            acc_lo = acc_lo + wr * lo
            acc_hi = acc_hi + wr * hi
        o_ref[:, c0:c0 + cw] = acc_lo.astype(o_ref.dtype)
        o_ref[:, half + c0:half + c0 + cw] = acc_hi.astype(o_ref.dtype)


def moe_combine(ys, pos, wgt_t, shared):
    m, d = shared.shape
    w = ys.shape[1]
    tm = _divisor(m, 128, V7X_LANES)
    return pl.pallas_call(
        functools.partial(_combine_kernel, cw=_divisor(w, 256, V7X_LANES)),
        grid=(m // tm,),
        in_specs=[pl.BlockSpec((TOP_K, tm), lambda i: (0, i), memory_space=pltpu.SMEM),
                  pl.BlockSpec((tm, TOP_K), lambda i: (i, 0)),
                  pl.BlockSpec((tm, d), lambda i: (i, 0)),
                  pl.BlockSpec(memory_space=pl.ANY)],
        out_specs=pl.BlockSpec((tm, d), lambda i: (i, 0)),
        out_shape=jax.ShapeDtypeStruct((m, d), BF16),
        scratch_shapes=[pltpu.VMEM((TOP_K, tm, w), jnp.uint32), pltpu.SemaphoreType.DMA(())],
        compiler_params=_params(("arbitrary",)),
        name="moe_combine",
    )(pos, wgt_t, shared, ys)


def _ffn_up_kernel(x_ref, wg_ref, wu_ref, o_ref):
    x = x_ref[...]
    g = jnp.dot(x, wg_ref[...], preferred_element_type=F32)
    u = jnp.dot(x, wu_ref[...], preferred_element_type=F32)
    o_ref[...] = (g * jax.nn.sigmoid(g) * u).astype(o_ref.dtype)


def ffn_up(h, w_gate, w_up, layer):
    m, d = h.shape
    ff = w_gate.shape[2]
    tm = _divisor(m, 768, 16)
    wspec = pl.BlockSpec((None, d, ff), lambda i: (layer, 0, 0))
    return pl.pallas_call(
        _ffn_up_kernel,
        grid=(m // tm,),
        in_specs=[pl.BlockSpec((tm, d), lambda i: (i, 0)), wspec, wspec],
        out_specs=pl.BlockSpec((tm, ff), lambda i: (i, 0)),
        out_shape=jax.ShapeDtypeStruct((m, ff), BF16),
        compiler_params=_params(("parallel",)),
        name="ffn_up",
    )(h, w_gate, w_up)


def _pos_kernel(eidx_ref, rank_ref, start_ref, pos_ref):
    iota_e = lax.broadcasted_iota(jnp.int32, (start_ref.shape[0], eidx_ref.shape[1]), 0)
    for r in range(TOP_K):
        first_row = jnp.sum(jnp.where(iota_e == eidx_ref[r:r + 1, :], start_ref[...], 0), axis=0, keepdims=True)
        pos_ref[r:r + 1, :] = rank_ref[r:r + 1, :] + first_row


def moe_pos(eidx, rank, row_start):
    _, m = eidx.shape
    n_e = row_start.shape[0]
    tm = _divisor(m, 512, V7X_LANES)
    spec = pl.BlockSpec((TOP_K, tm), lambda i: (0, i))
    return pl.pallas_call(
        _pos_kernel,
        grid=(m // tm,),
        in_specs=[spec, spec, pl.BlockSpec((n_e, 1), lambda i: (0, 0))],
        out_specs=spec,
        out_shape=jax.ShapeDtypeStruct((TOP_K, m), jnp.int32),
        compiler_params=_params(("parallel",)),
        name="moe_pos",
    )(eidx, rank, row_start.reshape(n_e, 1))


def moe_sparse(h2, h2p, eidx, wgt, w_gate, w_up, w_down, ws_gate, ws_up, ws_down, layer):
    m, d = h2.shape
    n_e = w_gate.shape[1]
    tr = MOE_TILE
    rank, cnt = moe_rank(eidx, n_e)
    counts = cnt[:, 0].astype(jnp.int32)
    tiles_per = (counts + (tr - 1)) // tr
    tile_end = jnp.cumsum(tiles_per)
    tile_start = tile_end - tiles_per
    pos = moe_pos(eidx, rank, tile_start * tr)
    n_tiles = (TOP_K * m) // tr + n_e
    t_idx = jnp.arange(n_tiles, dtype=jnp.int32)
    tile_expert = jnp.minimum(jnp.sum((tile_end[None, :] <= t_idx[:, None]).astype(jnp.int32), axis=1), n_e - 1)
    owner = tile_expert[:, None] == jnp.arange(n_e, dtype=jnp.int32)[None, :]
    left = jnp.sum(jnp.where(owner, counts[None, :] - (t_idx[:, None] - tile_start[None, :]) * tr, 0), axis=1)
    tile_rows = jnp.clip(left, 0, tr).astype(jnp.int32)
    pad_info = jnp.concatenate([tile_start * tr + counts, tiles_per * tr - counts, tile_end[-1:]]).astype(jnp.int32)
    xs = moe_dispatch(h2p, pos, pad_info, n_tiles * tr, n_e)
    ys = moe_ffn(xs, tile_expert, tile_rows, w_gate, w_up, w_down, layer)
    shared = matmul(ffn_up(h2, ws_gate, ws_up, layer), ws_down, layer)
    return moe_combine(ys, pos, wgt.T, shared)


def _router_split(router_w):
    rw_t = router_w.T
    rw_hi = rw_t.astype(BF16)
    rw_lo = (rw_t - rw_hi.astype(F32)).astype(BF16)
    return rw_hi, rw_lo


def kernel(x, c, ctx, c_ctx, ada_w, ada_b, norm_g, w_in, lam_vec, att_g, conv_w,
           ssm_a_re, ssm_a_im, ssm_log_dt, ssm_b_re, ssm_b_im, ssm_c_re, ssm_c_im, ssm_d,
           glu_w, glu_b, p_att, p_conv, p_ssm, w_o, router_w, router_b,
           exp_w_gate, exp_w_up, exp_w_down, sh_w_gate, sh_w_up, sh_w_down):
    bsz, n_lat, d = x.shape
    n_ctx = ctx.shape[1]
    t = n_ctx + n_lat
    m = bsz * t
    depth = ada_w.shape[0]
    att_w = p_att.shape[1]
    heads = att_w // ATT_VDIM
    conv_wd = conv_w.shape[-1]
    ssm_wd = ssm_d.shape[-1]
    n_e = router_w.shape[-1]
    n_p = ssm_a_re.shape[-1]
    assert bsz + 1 <= 8 and n_ctx % SSM_CHUNK == 0 and n_lat % SSM_CHUNK == 0 and n_lat % GRID_W == 0
    assert V7X_LANES % n_p == 0 and n_e % N_EXPERT_GROUPS == 0
    k_off = att_w
    v_off = 2 * att_w
    cb_off = 3 * att_w
    su_off = cb_off + 3 * conv_wd
    g_off = su_off + ssm_wd

    stream = (ctx, x)
    cond_cols = jnp.zeros((d, 8), F32).at[:, :bsz].set(c.T).at[:, bsz].set(c_ctx)
    tables = rope_tables(n_ctx, n_lat)
    ssm_groups = ssm_a_re.shape[2]
    perm = chunk_permutation(ssm_wd // ssm_groups)
    w_in_b, p_att_b, p_conv_b, p_ssm_b, w_o_b, glu_w_b, ws_gate_b, ws_up_b, ws_down_b = (
        a.astype(BF16) for a in (w_in, p_att, p_conv, p_ssm, w_o, glu_w, sh_w_gate, sh_w_up, sh_w_down))

    pending = None
    for i in range(depth):
        lam_init = 0.8 - 0.6 * math.exp(-0.3 * i)
        mod8 = modulation(cond_cols, ada_w, ada_b, bsz + 1, i).reshape(8, N_MOD, d)
        mod = jnp.stack([jnp.broadcast_to(mod8[bsz], (bsz, N_MOD, d)), mod8[:bsz]], axis=1)
        gains = norm_g[i]

        if pending is None:
            (h,) = resid_adaln(stream, None, mod, gains, ctx_len=n_ctx, pre_idx=0, shift_idx=0, scale_idx=1)
        else:
            branch, p_mod, p_gains = pending
            stream, h = resid_adaln(stream, branch, jnp.concatenate([mod, p_mod], axis=2),
                                    jnp.concatenate([gains, p_gains], axis=0), ctx_len=n_ctx,
                                    post_idx=4 + 3, gate_idx=N_MOD + 5, pre_idx=0, shift_idx=0, scale_idx=1)
        z = matmul(h.reshape(m, d), w_in_b, i)
        z3 = z.reshape(bsz, t, -1)

        qr, kr = rope_qk(z3, tables, att_w, n_ctx)
        att = jnp.zeros((bsz, t, att_w), BF16)
        att = diff_attention(qr, kr, z3, lam_vec[i], att_g[i], v_col_blk=v_off // ATT_VDIM, q_row0=0, n_q=n_ctx,
                             n_kv=n_ctx, lam_init=lam_init, out=att, heads=heads)
        att = diff_attention(qr, kr, z3, lam_vec[i], att_g[i], v_col_blk=v_off // ATT_VDIM, q_row0=n_ctx, n_q=n_lat,
                             n_kv=t, lam_init=lam_init, out=att, heads=heads)

        conv = short_conv(z3, conv_w[i], cb_off=cb_off, ctx_len=n_ctx)

        mats = ssm_matrices(ssm_a_re[i], ssm_a_im[i], ssm_log_dt[i], ssm_b_re[i], ssm_b_im[i],
                            ssm_c_re[i], ssm_c_im[i], ssm_d[i], V7X_LANES // n_p)
        ug = ssm_chunk_layout(z, perm, col_off=su_off, width=ssm_wd, n_g=ssm_groups)
        yg = ssm_branch(ug, mats, bsz=bsz, ctx_len=n_ctx)
        ssm = s5_glu(ssm_unchunk(yg, perm.T), glu_w_b, glu_b[i], i)

        merged = branch_merge(att.reshape(m, att_w), conv.reshape(m, conv_wd), ssm, z,
                              p_att_b, p_conv_b, p_ssm_b, i, g_off=g_off)
        o = matmul(merged, w_o_b, i).reshape(bsz, t, d)

        stream, h2, h2p, logits_t = resid_adaln(stream, o, mod, gains, ctx_len=n_ctx, post_idx=1, gate_idx=2,
                                                pre_idx=2, shift_idx=3, scale_idx=4,
                                                router=_router_split(router_w[i]))
        eidx, wgt = route(logits_t, router_b[i])
        f = moe_sparse(h2.reshape(m, d), h2p.reshape(m, d // 2), eidx, wgt, exp_w_gate, exp_w_up, exp_w_down,
                       ws_gate_b, ws_up_b, ws_down_b, i).reshape(bsz, t, d)
        pending = (f, mod, gains)

    branch, p_mod, p_gains = pending
    (out,) = resid_adaln(stream, branch, p_mod, p_gains, ctx_len=n_ctx, post_idx=3, gate_idx=5,
                         latent_only_out=True)
    return out
```
